```python
import math
import jax, jax.numpy as jnp
from jax import lax
import numpy as np

D_MODEL = 1024
BATCH = 4
SEQ = 4096
DEPTH = 4
DEC_BATCH = 128
DEC_SEQ = 4
PAST_LEN = 2048
PAGE_SIZE = 128

N_A = DEPTH // 2
N_B = DEPTH - N_A
MEM_TOKENS = 256
MEM_HEADS = 4
MEM_HEAD_DIM = D_MODEL // 16
MEM_WIDTH = MEM_HEADS * MEM_HEAD_DIM
MIX_WIDTH = D_MODEL - MEM_WIDTH
POOL_WINDOWS = (2, 4, 8, 16)
POOL_GROUPS = len(POOL_WINDOWS)
GROUP_WIDTH = MIX_WIDTH // POOL_GROUPS
POOL_HIST = max(POOL_WINDOWS) - 1
DIFF_HEAD_DIM = 64
DIFF_HEADS = MIX_WIDTH // (2 * DIFF_HEAD_DIM)
DIFF_V_DIM = 2 * DIFF_HEAD_DIM
D_FF = 4 * D_MODEL
ROPE_THETA = 10000.0
Q_BLOCK = 128
RMS_EPS = 1e-6
NEG_INF = -1e30

kernel_name = "yoco_pool_diffattn_mem_decoder_step"


def rmsnorm(x, g):
    xf = x.astype(jnp.float32)
    y = xf * lax.rsqrt(jnp.mean(xf * xf, axis=-1, keepdims=True) + RMS_EPS)
    return (y * g.astype(jnp.float32)).astype(x.dtype)


def rope(x, pos):
    d = x.shape[-1]
    inv = 1.0 / (ROPE_THETA ** (jnp.arange(0, d, 2, dtype=jnp.float32) / d))
    ang = pos.astype(jnp.float32)[:, None] * inv[None, :]
    shape = (1, pos.shape[0]) + (1,) * (x.ndim - 3) + (d,)
    cos = jnp.concatenate([jnp.cos(ang), jnp.cos(ang)], axis=-1).reshape(shape)
    sin = jnp.concatenate([jnp.sin(ang), jnp.sin(ang)], axis=-1).reshape(shape)
    xf = x.astype(jnp.float32)
    rot = jnp.concatenate([-xf[..., d // 2:], xf[..., :d // 2]], axis=-1)
    return (xf * cos + rot * sin).astype(x.dtype)


def multi_scale_pool(u, pos):
    L = u.shape[1]
    c = jnp.cumsum(u.astype(jnp.float32), axis=1)
    outs = []
    for g, w in enumerate(POOL_WINDOWS):
        cg = c[:, :, g]
        prev = jnp.pad(cg, ((0, 0), (w, 0), (0, 0)))[:, :L]
        cnt = jnp.minimum(w, pos + 1).astype(jnp.float32)
        outs.append((cg - prev) / cnt[None, :, None])
    mean = jnp.stack(outs, axis=2)
    return (mean - u.astype(jnp.float32)).astype(u.dtype)


def diff_core(q, k, v, q_pos, k_pos, lam):
    s = jnp.einsum('bqhjd,bkhjd->bhjqk', q, k, preferred_element_type=jnp.float32) * (DIFF_HEAD_DIM ** -0.5)
    mask = k_pos[None, :] <= q_pos[:, None]
    s = jnp.where(mask[None, None, None], s, NEG_INF)
    p = jax.nn.softmax(s, axis=-1)
    a = p[:, :, 0] - lam * p[:, :, 1]
    return jnp.einsum('bhqk,bkhe->bqhe', a.astype(v.dtype), v)


def memory_attention(q, mk, mv):
    s = jnp.einsum('bqhd,bmhd->bhqm', q, mk, preferred_element_type=jnp.float32) * (MEM_HEAD_DIM ** -0.5)
    p = jax.nn.softmax(s, axis=-1)
    return jnp.einsum('bhqm,bmhd->bqhd', p.astype(mv.dtype), mv)


def trunk(x, pos, pool_hist, mem_k, mem_v, past_k, past_v, p):
    B, T, _ = x.shape
    pool_new = []
    k_new = v_new = k_all = v_all = None
    for l in range(DEPTH):
        if l == N_A:
            hk = rmsnorm(x, p['kv_norm'])
            k_new = rope((hk @ p['w_k']).reshape(B, T, DIFF_HEADS, 2, DIFF_HEAD_DIM), pos)
            v_new = (hk @ p['w_v']).reshape(B, T, DIFF_HEADS, DIFF_V_DIM)
            if past_k is None:
                k_all, v_all = k_new, v_new
            else:
                k_all = jnp.concatenate([past_k, k_new], axis=1)
                v_all = jnp.concatenate([past_v, v_new], axis=1)
        h = rmsnorm(x, p['norm1'][l])
        z = h @ p['w_in'][l]
        z_mix, z_mem = z[..., :MIX_WIDTH], z[..., MIX_WIDTH:]
        if l < N_A:
            u = z_mix.reshape(B, T, POOL_GROUPS, GROUP_WIDTH)
            if pool_hist is None:
                ext, pos_ext = u, pos
            else:
                hist = pool_hist[l].reshape(B, POOL_HIST, POOL_GROUPS, GROUP_WIDTH)
                ext = jnp.concatenate([hist, u], axis=1)
                pos_ext = jnp.concatenate([pos[0] - POOL_HIST + jnp.arange(POOL_HIST, dtype=pos.dtype), pos])
            pooled = multi_scale_pool(ext, pos_ext)[:, -T:]
            pool_new.append(ext[:, -POOL_HIST:].reshape(B, POOL_HIST, MIX_WIDTH))
            mix = jnp.einsum('btgc,gcd->btgd', pooled, p['w_pool'][l]).reshape(B, T, MIX_WIDTH) * p['pool_scale'][l]
        else:
            j = l - N_A
            lam_init = 0.8 - 0.6 * math.exp(-0.3 * l)
            lam = (jnp.exp(jnp.sum(p['lambda_q1'][j].astype(jnp.float32) * p['lambda_k1'][j].astype(jnp.float32)))
                   - jnp.exp(jnp.sum(p['lambda_q2'][j].astype(jnp.float32) * p['lambda_k2'][j].astype(jnp.float32)))
                   + lam_init)
            q = rope(z_mix.reshape(B, T, DIFF_HEADS, 2, DIFF_HEAD_DIM), pos)
            k_pos = jnp.arange(k_all.shape[1])
            if past_k is None and T % Q_BLOCK == 0 and T >= Q_BLOCK:
                nblk = T // Q_BLOCK
                qb = q.reshape(B, nblk, Q_BLOCK, DIFF_HEADS, 2, DIFF_HEAD_DIM).transpose(1, 0, 2, 3, 4, 5)
                pb = pos.reshape(nblk, Q_BLOCK)
                ob = lax.map(lambda a: diff_core(a[0], k_all, v_all, a[1], k_pos, lam), (qb, pb))
                o = ob.transpose(1, 0, 2, 3, 4).reshape(B, T, DIFF_HEADS, DIFF_V_DIM)
            else:
                o = diff_core(q, k_all, v_all, pos, k_pos, lam)
            o = rmsnorm(o, p['subln'][j]) * (1.0 - lam_init)
            mix = o.reshape(B, T, MIX_WIDTH)
        mem_out = memory_attention(z_mem.reshape(B, T, MEM_HEADS, MEM_HEAD_DIM), mem_k[l], mem_v[l]).reshape(B, T, MEM_WIDTH)
        x = x + jnp.concatenate([mix, mem_out], axis=-1) @ p['w_out'][l]
        h2 = rmsnorm(x, p['norm2'][l])
        x = x + jnp.square(jax.nn.relu(h2 @ p['w_mlp1'][l])) @ p['w_mlp2'][l]
    y = rmsnorm(x, p['final_norm'])
    return y, jnp.stack(pool_new, axis=0), k_new, v_new


def setup_inputs(seed: int = 0) -> dict:
    key = jax.random.key(seed)
    ks = jax.random.split(key, 32)
    f32 = jnp.float32
    n_pages = PAST_LEN // PAGE_SIZE
    n_used = DEC_BATCH * n_pages
    n_pool = n_used + n_used // 4
    page_table = jax.random.permutation(ks[0], n_pool)[:n_used].reshape(DEC_BATCH, n_pages).astype(jnp.int32)
    nrm = lambda k, s, sc: jax.random.normal(k, s, f32) * sc
    return {
        'x_prompt': nrm(ks[1], (BATCH, SEQ, D_MODEL), 1.0),
        'x_sample': nrm(ks[2], (DEC_BATCH, DEC_SEQ, D_MODEL), 1.0),
        'state_pool': nrm(ks[3], (N_A, DEC_BATCH, POOL_HIST, MIX_WIDTH), 1.0),
        'cache_mem_k': nrm(ks[4], (DEPTH, DEC_BATCH, MEM_TOKENS, MEM_HEADS, MEM_HEAD_DIM), 1.0),
        'cache_mem_v': nrm(ks[5], (DEPTH, DEC_BATCH, MEM_TOKENS, MEM_HEADS, MEM_HEAD_DIM), 1.0),
        'cache_k': nrm(ks[6], (n_pool, PAGE_SIZE, DIFF_HEADS, 2, DIFF_HEAD_DIM), 1.0),
        'cache_v': nrm(ks[7], (n_pool, PAGE_SIZE, DIFF_HEADS, DIFF_V_DIM), 1.0),
        'page_table': page_table,
        'mem_prompt': nrm(ks[8], (BATCH, MEM_TOKENS, D_MODEL), 1.0),
        'norm1': 1.0 + nrm(ks[9], (DEPTH, D_MODEL), 0.05),
        'norm2': 1.0 + nrm(ks[10], (DEPTH, D_MODEL), 0.05),
        'final_norm': 1.0 + nrm(ks[11], (D_MODEL,), 0.05),
        'kv_norm': 1.0 + nrm(ks[12], (D_MODEL,), 0.05),
        'mem_norm': 1.0 + nrm(ks[13], (D_MODEL,), 0.05),
        'w_in': nrm(ks[14], (DEPTH, D_MODEL, D_MODEL), D_MODEL ** -0.5),
        'w_out': nrm(ks[15], (DEPTH, D_MODEL, D_MODEL), D_MODEL ** -0.5),
        'w_mem_k': nrm(ks[16], (DEPTH, D_MODEL, MEM_WIDTH), D_MODEL ** -0.5),
        'w_mem_v': nrm(ks[17], (DEPTH, D_MODEL, MEM_WIDTH), D_MODEL ** -0.5),
        'w_pool': nrm(ks[18], (N_A, POOL_GROUPS, GROUP_WIDTH, GROUP_WIDTH), GROUP_WIDTH ** -0.5),
        'pool_scale': 1.0 + nrm(ks[19], (N_A, MIX_WIDTH), 0.1),
        'w_k': nrm(ks[20], (D_MODEL, MIX_WIDTH), D_MODEL ** -0.5),
        'w_v': nrm(ks[21], (D_MODEL, MIX_WIDTH), D_MODEL ** -0.5),
        'lambda_q1': nrm(ks[22], (N_B, DIFF_HEAD_DIM), 0.1),
        'lambda_k1': nrm(ks[23], (N_B, DIFF_HEAD_DIM), 0.1),
        'lambda_q2': nrm(ks[24], (N_B, DIFF_HEAD_DIM), 0.1),
        'lambda_k2': nrm(ks[25], (N_B, DIFF_HEAD_DIM), 0.1),
        'subln': 1.0 + nrm(ks[26], (N_B, DIFF_V_DIM), 0.05),
        'w_mlp1': nrm(ks[27], (DEPTH, D_MODEL, D_FF), D_MODEL ** -0.5),
        'w_mlp2': nrm(ks[28], (DEPTH, D_FF, D_MODEL), D_FF ** -0.5),
    }


def reference(x_prompt, x_sample, state_pool, cache_mem_k, cache_mem_v, cache_k, cache_v, page_table, mem_prompt,
              norm1, norm2, final_norm, kv_norm, mem_norm, w_in, w_out, w_mem_k, w_mem_v, w_pool, pool_scale,
              w_k, w_v, lambda_q1, lambda_k1, lambda_q2, lambda_k2, subln, w_mlp1, w_mlp2):
    p = dict(norm1=norm1, norm2=norm2, final_norm=final_norm, kv_norm=kv_norm, w_in=w_in, w_out=w_out,
             w_pool=w_pool, pool_scale=pool_scale, w_k=w_k, w_v=w_v, lambda_q1=lambda_q1, lambda_k1=lambda_k1,
             lambda_q2=lambda_q2, lambda_k2=lambda_k2, subln=subln, w_mlp1=w_mlp1, w_mlp2=w_mlp2)
    Bp, Tp, _ = x_prompt.shape
    Bs, Ts, _ = x_sample.shape
    hm = rmsnorm(mem_prompt, mem_norm)
    mem_k_prompt = jnp.einsum('bmd,lde->lbme', hm, w_mem_k).reshape(DEPTH, Bp, MEM_TOKENS, MEM_HEADS, MEM_HEAD_DIM)
    mem_v_prompt = jnp.einsum('bmd,lde->lbme', hm, w_mem_v).reshape(DEPTH, Bp, MEM_TOKENS, MEM_HEADS, MEM_HEAD_DIM)
    pos_p = jnp.arange(Tp, dtype=jnp.int32)
    y_prompt, pool_prompt, k_prompt, v_prompt = trunk(x_prompt, pos_p, None, mem_k_prompt, mem_v_prompt, None, None, p)
    n_pages = page_table.shape[1]
    past_k = cache_k[page_table].reshape(Bs, n_pages * PAGE_SIZE, DIFF_HEADS, 2, DIFF_HEAD_DIM)
    past_v = cache_v[page_table].reshape(Bs, n_pages * PAGE_SIZE, DIFF_HEADS, DIFF_V_DIM)
    pos_s = PAST_LEN + jnp.arange(Ts, dtype=jnp.int32)
    y_sample, pool_sample, k_sample, v_sample = trunk(x_sample, pos_s, state_pool, cache_mem_k, cache_mem_v, past_k, past_v, p)
    return (y_prompt, y_sample, pool_prompt, mem_k_prompt, mem_v_prompt, k_prompt, v_prompt, pool_sample, k_sample, v_sample)
```

```python
import functools
import math

import jax
import jax.numpy as jnp
from jax import lax
from jax.experimental import pallas as pl
from jax.experimental.pallas import tpu as pltpu

D_MODEL = 1024
DEPTH = 4
N_A = DEPTH // 2
N_B = DEPTH - N_A
PAST_LEN = 2048
PAGE_SIZE = 128
MEM_TOKENS = 256
MEM_HEADS = 4
MEM_HEAD_DIM = D_MODEL // 16
MEM_WIDTH = MEM_HEADS * MEM_HEAD_DIM
MIX_WIDTH = D_MODEL - MEM_WIDTH
POOL_WINDOWS = (2, 4, 8, 16)
GROUP_WIDTH = MIX_WIDTH // len(POOL_WINDOWS)
POOL_HIST = max(POOL_WINDOWS) - 1
POOL_PAD = POOL_HIST + 1
DIFF_HEAD_DIM = 64
DIFF_HEADS = MIX_WIDTH // (2 * DIFF_HEAD_DIM)
DIFF_V_DIM = 2 * DIFF_HEAD_DIM
D_FF = 4 * D_MODEL
ROPE_THETA = 10000.0
RMS_EPS = 1e-6
NEG_INF = -1e30

LANES = 128
FF_CHUNK = 1024
VMEM_LIMIT = 56 * 1024 * 1024

BF16 = jnp.bfloat16
F32 = jnp.float32


def _cparams(n_axes):
    return pltpu.CompilerParams(dimension_semantics=("arbitrary",) * n_axes,
                                vmem_limit_bytes=VMEM_LIMIT)


def _const_spec(shape):
    nd = len(shape)
    return pl.BlockSpec(shape, lambda *_: (0,) * nd)


def _rms_unit(x):
    return x * lax.rsqrt(jnp.mean(x * x, axis=-1, keepdims=True) + RMS_EPS)


def _rope_slab(x, cos, sin_lo, sin_hi):
    return (x * cos + pltpu.roll(x, LANES - DIFF_HEAD_DIM // 2, 1) * sin_lo
            + pltpu.roll(x, DIFF_HEAD_DIM // 2, 1) * sin_hi)


def _lambda_value(lq1, lk1, lq2, lk2, j, lam_init):
    a = jnp.sum(lq1[j:j + 1, :] * lk1[j:j + 1, :], axis=-1, keepdims=True)
    b = jnp.sum(lq2[j:j + 1, :] * lk2[j:j + 1, :], axis=-1, keepdims=True)
    return jnp.exp(a) - jnp.exp(b) + lam_init


def _norm_mm_kernel(x_ref, g_ref, w_ref, o_ref):
    h = _rms_unit(x_ref[...]) * g_ref[...]
    o_ref[...] = jnp.dot(h.astype(BF16), w_ref[...], preferred_element_type=F32)


def _norm_mm(x, g, w, tm):
    m, d = x.shape
    n = w.shape[1]
    return pl.pallas_call(
        _norm_mm_kernel,
        grid=(m // tm,),
        in_specs=[pl.BlockSpec((tm, d), lambda i: (i, 0)),
                  _const_spec((1, d)),
                  _const_spec((d, n))],
        out_specs=pl.BlockSpec((tm, n), lambda i: (i, 0)),
        out_shape=jax.ShapeDtypeStruct((m, n), F32),
        compiler_params=_cparams(1),
        name="norm_mm",
    )(x, g.reshape(1, d), w)


def _kv_kernel(x_ref, gkv_ref, gn_ref, wk_ref, wv_ref, win_ref, cos_ref, slo_ref, shi_ref,
               k_ref, v_ref, z_ref, *bf_refs):
    y = _rms_unit(x_ref[...])
    hk = (y * gkv_ref[...]).astype(BF16)
    hn = (y * gn_ref[...]).astype(BF16)
    z_ref[...] = jnp.dot(hn, win_ref[...], preferred_element_type=F32)
    v = jnp.dot(hk, wv_ref[...], preferred_element_type=F32)
    v_ref[...] = v
    kraw = jnp.dot(hk, wk_ref[...], preferred_element_type=F32)
    cos, slo, shi = cos_ref[...], slo_ref[...], shi_ref[...]
    if bf_refs:
        kt_ref, vb_ref = bf_refs
        vb_ref[...] = v.astype(BF16)
    for h in range(DIFF_HEADS):
        sl = slice(h * LANES, (h + 1) * LANES)
        kr = _rope_slab(kraw[:, sl], cos, slo, shi)
        k_ref[:, sl] = kr
        if bf_refs:
            kt_ref[0, sl, :] = kr.T.astype(BF16)


def _kv_proj(x, gkv, gn, wk, wv, win, tabs, tm, with_bf16):
    m, d = x.shape
    n_tab = tabs[0].shape[0] // tm
    tab_spec = pl.BlockSpec((tm, LANES), lambda i: (i % n_tab, 0))
    row = lambda n: pl.BlockSpec((tm, n), lambda i: (i, 0))
    out_specs = [row(MIX_WIDTH), row(MIX_WIDTH), row(D_MODEL)]
    out_shape = [jax.ShapeDtypeStruct((m, MIX_WIDTH), F32),
                 jax.ShapeDtypeStruct((m, MIX_WIDTH), F32),
                 jax.ShapeDtypeStruct((m, D_MODEL), F32)]
    if with_bf16:
        out_specs += [pl.BlockSpec((1, MIX_WIDTH, tm), lambda i: (i, 0, 0)), row(MIX_WIDTH)]
        out_shape += [jax.ShapeDtypeStruct((m // tm, MIX_WIDTH, tm), BF16),
                      jax.ShapeDtypeStruct((m, MIX_WIDTH), BF16)]
    return pl.pallas_call(
        _kv_kernel,
        grid=(m // tm,),
        in_specs=[row(d), _const_spec((1, d)), _const_spec((1, d)),
                  _const_spec((d, MIX_WIDTH)), _const_spec((d, MIX_WIDTH)), _const_spec((d, D_MODEL)),
                  tab_spec, tab_spec, tab_spec],
        out_specs=out_specs,
        out_shape=out_shape,
        compiler_params=_cparams(1),
        name="kv_proj",
    )(x, gkv.reshape(1, d), gn.reshape(1, d), wk, wv, win, *tabs)


def _pool_mix(ext, inv_cnt, w_ref, scale_ref):
    s2 = ext + pltpu.roll(ext, 1, 0)
    s4 = s2 + pltpu.roll(s2, 2, 0)
    s8 = s4 + pltpu.roll(s4, 4, 0)
    s16 = s8 + pltpu.roll(s8, 8, 0)
    lane = lax.broadcasted_iota(jnp.int32, (1, MIX_WIDTH), 1)
    mean = jnp.where(lane < GROUP_WIDTH, s2 * inv_cnt[0],
                     jnp.where(lane < 2 * GROUP_WIDTH, s4 * inv_cnt[1],
                               jnp.where(lane < 3 * GROUP_WIDTH, s8 * inv_cnt[2], s16 * inv_cnt[3])))
    pooled = (mean - ext)[POOL_PAD:]
    return jnp.dot(pooled.astype(BF16), w_ref[...], preferred_element_type=F32) * scale_ref[...]


def _pool_prompt_kernel(u_ref, h_ref, w_ref, scale_ref, o_ref, *, tiles_per_seq, tm):
    t0 = (pl.program_id(0) % tiles_per_seq) * tm
    hist = jnp.where(t0 > 0, h_ref[...], 0.0)
    ext = jnp.concatenate([hist, u_ref[...]], axis=0)
    pos = t0 - POOL_PAD + lax.broadcasted_iota(jnp.int32, (tm + POOL_PAD, 1), 0)
    inv_cnt = [1.0 / jnp.clip(pos + 1, 1, w).astype(F32) for w in POOL_WINDOWS]
    o_ref[...] = _pool_mix(ext, inv_cnt, w_ref, scale_ref).astype(o_ref.dtype)


def _pool_prompt(z, wbd, scale, seq, tm):
    m = z.shape[0]
    hist_blocks = tm // POOL_PAD
    return pl.pallas_call(
        functools.partial(_pool_prompt_kernel, tiles_per_seq=seq // tm, tm=tm),
        grid=(m // tm,),
        in_specs=[pl.BlockSpec((tm, MIX_WIDTH), lambda i: (i, 0)),
                  pl.BlockSpec((POOL_PAD, MIX_WIDTH), lambda i: (jnp.maximum(i * hist_blocks - 1, 0), 0)),
                  _const_spec((MIX_WIDTH, MIX_WIDTH)),
                  _const_spec((1, MIX_WIDTH))],
        out_specs=pl.BlockSpec((tm, MIX_WIDTH), lambda i: (i, 0)),
        out_shape=jax.ShapeDtypeStruct((m, MIX_WIDTH), BF16),
        compiler_params=_cparams(1),
        name="pool_prompt",
    )(z, z, wbd, scale.reshape(1, MIX_WIDTH))


def _pool_sample_kernel(ext_ref, w_ref, scale_ref, o_ref):
    inv_cnt = [1.0 / w for w in POOL_WINDOWS]
    ext = ext_ref[...]
    mix = _pool_mix(jnp.concatenate([ext[-POOL_PAD:], ext], axis=0), inv_cnt, w_ref, scale_ref)
    o_ref[...] = mix.astype(o_ref.dtype)


def _pool_sample(ext, wbd, scale, tm):
    m = ext.shape[0]
    return pl.pallas_call(
        _pool_sample_kernel,
        grid=(m // tm,),
        in_specs=[pl.BlockSpec((tm, MIX_WIDTH), lambda i: (i, 0)),
                  _const_spec((MIX_WIDTH, MIX_WIDTH)),
                  _const_spec((1, MIX_WIDTH))],
        out_specs=pl.BlockSpec((tm, MIX_WIDTH), lambda i: (i, 0)),
        out_shape=jax.ShapeDtypeStruct((m, MIX_WIDTH), BF16),
        compiler_params=_cparams(1),
        name="pool_sample",
    )(ext, wbd, scale.reshape(1, MIX_WIDTH))


def _softmax_rows(s):
    e = jnp.exp(s - jnp.max(s, axis=-1, keepdims=True))
    return e * (1.0 / jnp.sum(e, axis=-1, keepdims=True))


def _mem_prompt_kernel(q_ref, mk_ref, mv_ref, o_ref):
    q = q_ref[...] * (MEM_HEAD_DIM ** -0.5)
    mk = mk_ref[...].astype(BF16)
    mv = mv_ref[...]
    lane = lax.broadcasted_iota(jnp.int32, (1, MEM_WIDTH), 1)
    ps, mvs = [], []
    for h in range(MEM_HEADS):
        head = (lane >= h * MEM_HEAD_DIM) & (lane < (h + 1) * MEM_HEAD_DIM)
        qh = jnp.where(head, q, 0.0).astype(BF16)
        s = lax.dot_general(qh, mk, (((1,), (1,)), ((), ())), preferred_element_type=F32)
        ps.append(_softmax_rows(s).astype(BF16))
        mvs.append(jnp.where(head, mv, 0.0).astype(BF16))
    o = jnp.dot(jnp.concatenate(ps, axis=1), jnp.concatenate(mvs, axis=0), preferred_element_type=F32)
    o_ref[...] = o.astype(o_ref.dtype)


def _mem_prompt(z, memkv, layer, seq, tq):
    m = z.shape[0]
    tiles_per_seq = seq // tq
    q_col = MIX_WIDTH // MEM_WIDTH
    return pl.pallas_call(
        _mem_prompt_kernel,
        grid=(m // tq,),
        in_specs=[pl.BlockSpec((tq, MEM_WIDTH), lambda i: (i, q_col)),
                  pl.BlockSpec((MEM_TOKENS, MEM_WIDTH), lambda i: (i // tiles_per_seq, 2 * layer)),
                  pl.BlockSpec((MEM_TOKENS, MEM_WIDTH), lambda i: (i // tiles_per_seq, 2 * layer + 1))],
        out_specs=pl.BlockSpec((tq, MEM_WIDTH), lambda i: (i, 0)),
        out_shape=jax.ShapeDtypeStruct((m, MEM_WIDTH), BF16),
        compiler_params=_cparams(1),
        name="mem_prompt",
    )(z, memkv, memkv)


def _mem_sample_kernel(q_ref, mk_ref, mv_ref, o_ref):
    q = q_ref[...]
    mk = mk_ref[0].astype(BF16)
    mv = mv_ref[0].astype(BF16)
    s = jnp.einsum('bqd,bmd->bqm', q, mk, preferred_element_type=F32)
    p = _softmax_rows(s).astype(BF16)
    o_ref[...] = jnp.einsum('bqm,bmd->bqd', p, mv, preferred_element_type=F32)


def _mem_sample(q_rows, cache_k, cache_v, layer, bb):
    nb, rows, _ = q_rows.shape
    kv_spec = pl.BlockSpec((1, bb, MEM_TOKENS, MEM_WIDTH), lambda i: (layer, i, 0, 0))
    return pl.pallas_call(
        _mem_sample_kernel,
        grid=(nb // bb,),
        in_specs=[pl.BlockSpec((bb, rows, MEM_WIDTH), lambda i: (i, 0, 0)), kv_spec, kv_spec],
        out_specs=pl.BlockSpec((bb, rows, MEM_WIDTH), lambda i: (i, 0, 0)),
        out_shape=jax.ShapeDtypeStruct((nb, rows, MEM_WIDTH), F32),
        compiler_params=_cparams(1),
        name="mem_sample",
    )(q_rows, cache_k, cache_v)


def _subln(o, g_ref, lam_init):
    return _rms_unit(o) * g_ref[...] * (1.0 - lam_init)


def _diff_prompt_kernel(lq1, lk1, lq2, lk2, g_ref, q_ref, cos_ref, slo_ref, shi_ref, kt_ref, v_ref,
                        o_ref, qm_ref, m_ref, l_ref, acc_ref, *, j, lam_init, tq):
    qi = pl.program_id(2)
    q = _rope_slab(q_ref[...], cos_ref[...], slo_ref[...], shi_ref[...]) * (DIFF_HEAD_DIM ** -0.5)
    lane = lax.broadcasted_iota(jnp.int32, (1, LANES), 1)
    qm_ref[0] = jnp.where(lane < DIFF_HEAD_DIM, q, 0.0).astype(BF16)
    qm_ref[1] = jnp.where(lane >= DIFF_HEAD_DIM, q, 0.0).astype(BF16)
    m_ref[...] = jnp.full(m_ref.shape, NEG_INF, F32)
    l_ref[...] = jnp.zeros(l_ref.shape, F32)
    acc_ref[...] = jnp.zeros(acc_ref.shape, F32)

    def step(kv, masked):
        kt = kt_ref[0, kv]
        v = v_ref[pl.ds(pl.multiple_of(kv * tq, tq), tq), :]
        for a in range(2):
            s = jnp.dot(qm_ref[a], kt, preferred_element_type=F32)
            if masked:
                row = lax.broadcasted_iota(jnp.int32, (tq, tq), 0)
                col = lax.broadcasted_iota(jnp.int32, (tq, tq), 1)
                s = jnp.where(col <= row, s, NEG_INF)
            m_prev = m_ref[a]
            m_new = jnp.maximum(m_prev, jnp.max(s, axis=-1, keepdims=True))
            alpha = jnp.exp(m_prev - m_new)
            p = jnp.exp(s - m_new)
            l_ref[a] = alpha * l_ref[a] + jnp.sum(p, axis=-1, keepdims=True)
            acc_ref[a] = alpha * acc_ref[a] + jnp.dot(p.astype(BF16), v, preferred_element_type=F32)
            m_ref[a] = m_new

    def body(kv, carry):
        step(kv, False)
        return carry

    lax.fori_loop(0, qi, body, 0)
    step(qi, True)

    lam = _lambda_value(lq1[...], lk1[...], lq2[...], lk2[...], j, lam_init)
    o = acc_ref[0] * (1.0 / l_ref[0]) - lam * (acc_ref[1] * (1.0 / l_ref[1]))
    o_ref[...] = _subln(o, g_ref, lam_init).astype(o_ref.dtype)


def _diff_prompt(z, kt, vb, tabs, lams, g, j, lam_init, batch, seq, tq):
    m = z.shape[0]
    nq = seq // tq
    lam_spec = _const_spec((N_B, DIFF_HEAD_DIM))
    tab_spec = pl.BlockSpec((tq, LANES), lambda b, h, i: (i, 0))
    return pl.pallas_call(
        functools.partial(_diff_prompt_kernel, j=j, lam_init=lam_init, tq=tq),
        grid=(batch, DIFF_HEADS, nq),
        in_specs=[lam_spec, lam_spec, lam_spec, lam_spec,
                  _const_spec((1, DIFF_V_DIM)),
                  pl.BlockSpec((tq, LANES), lambda b, h, i: (b * nq + i, h)),
                  tab_spec, tab_spec, tab_spec,
                  pl.BlockSpec((1, nq, LANES, tq), lambda b, h, i: (b, 0, h, 0)),
                  pl.BlockSpec((seq, LANES), lambda b, h, i: (b, h))],
        out_specs=pl.BlockSpec((tq, LANES), lambda b, h, i: (b * nq + i, h)),
        out_shape=jax.ShapeDtypeStruct((m, MIX_WIDTH), BF16),
        scratch_shapes=[pltpu.VMEM((2, tq, LANES), BF16),
                        pltpu.VMEM((2, tq, 1), F32),
                        pltpu.VMEM((2, tq, 1), F32),
                        pltpu.VMEM((2, tq, LANES), F32)],
        compiler_params=_cparams(3),
        name="diff_prompt",
    )(*lams, g.reshape(1, DIFF_V_DIM), z, *tabs, kt.reshape(batch, nq, MIX_WIDTH, tq), vb)


def _diff_sample_kernel(pt_ref, lq1, lk1, lq2, lk2, g_ref, z_ref, cos_ref, slo_ref, shi_ref,
                        kn_ref, vn_ref, *rest, j, lam_init, n_pages, n_tok):
    del pt_ref
    k_pages = rest[:n_pages]
    v_pages = rest[n_pages:2 * n_pages]
    o_ref, x_scr, pg_scr, kb, vb = rest[2 * n_pages:]
    past = n_pages * PAGE_SIZE
    rows = 2 * n_tok

    x_scr[...] = jnp.zeros(x_scr.shape, F32)
    x_scr[0:n_tok, :] = z_ref[0]
    x = x_scr[...]
    x = x + pltpu.roll(x, n_tok, 0)

    for r in range(n_pages):
        kb[r * PAGE_SIZE:(r + 1) * PAGE_SIZE, :] = k_pages[r][0].astype(BF16)
        vb[r * PAGE_SIZE:(r + 1) * PAGE_SIZE, :] = v_pages[r][0].astype(BF16)
    for new_ref, dst in ((kn_ref, kb), (vn_ref, vb)):
        pg_scr[...] = jnp.zeros(pg_scr.shape, F32)
        pg_scr[0:n_tok, :] = new_ref[0]
        dst[past:past + PAGE_SIZE, :] = pg_scr[...].astype(BF16)

    lam = _lambda_value(lq1[...], lk1[...], lq2[...], lk2[...], j, lam_init)
    row = lax.broadcasted_iota(jnp.int32, (rows, LANES), 0)
    lane = lax.broadcasted_iota(jnp.int32, (rows, LANES), 1)
    own_map = (row < n_tok) == (lane < DIFF_HEAD_DIM)
    n_keys = past + PAGE_SIZE
    key = lax.broadcasted_iota(jnp.int32, (rows, n_keys), 1)
    tok = lax.broadcasted_iota(jnp.int32, (rows, n_keys), 0) % n_tok
    visible = key - past <= tok
    cos, slo, shi = cos_ref[...], slo_ref[...], shi_ref[...]

    for h in range(DIFF_HEADS):
        sl = slice(h * LANES, (h + 1) * LANES)
        q = _rope_slab(x[:, sl], cos, slo, shi) * (DIFF_HEAD_DIM ** -0.5)
        q = jnp.where(own_map, q, 0.0).astype(BF16)
        s = lax.dot_general(q, kb[:, sl], (((1,), (1,)), ((), ())), preferred_element_type=F32)
        p = _softmax_rows(jnp.where(visible, s, NEG_INF))
        a = p - lam * pltpu.roll(p, n_tok, 0)
        o = jnp.dot(a.astype(BF16), vb[:, sl], preferred_element_type=F32)
        o_ref[0, :, sl] = _subln(o[0:n_tok], g_ref, lam_init).astype(o_ref.dtype)


def _diff_sample(page_table, z3, tabs, k_new, v_new, cache_k, cache_v, lams, g, j, lam_init):
    nb, n_tok, _ = z3.shape
    n_pages = page_table.shape[1]
    rows = 2 * n_tok
    lam_spec = _const_spec((N_B, DIFF_HEAD_DIM))
    tab_spec = _const_spec((rows, LANES))
    tok_spec = pl.BlockSpec((1, n_tok, MIX_WIDTH), lambda b, pt: (b, 0, 0))

    def page_spec(r):
        return pl.BlockSpec((1, PAGE_SIZE, MIX_WIDTH), lambda b, pt: (pt[b, r], 0, 0))

    grid_spec = pltpu.PrefetchScalarGridSpec(
        num_scalar_prefetch=1,
        grid=(nb,),
        in_specs=[lam_spec, lam_spec, lam_spec, lam_spec, _const_spec((1, DIFF_V_DIM)),
                  tok_spec, tab_spec, tab_spec, tab_spec, tok_spec, tok_spec]
                 + [page_spec(r) for r in range(n_pages)] * 2,
        out_specs=tok_spec,
        scratch_shapes=[pltpu.VMEM((rows, MIX_WIDTH), F32),
                        pltpu.VMEM((PAGE_SIZE, MIX_WIDTH), F32),
                        pltpu.VMEM(((n_pages + 1) * PAGE_SIZE, MIX_WIDTH), BF16),
                        pltpu.VMEM(((n_pages + 1) * PAGE_SIZE, MIX_WIDTH), BF16)],
    )
    return pl.pallas_call(
        functools.partial(_diff_sample_kernel, j=j, lam_init=lam_init, n_pages=n_pages, n_tok=n_tok),
        grid_spec=grid_spec,
        out_shape=jax.ShapeDtypeStruct((nb, n_tok, MIX_WIDTH), BF16),
        compiler_params=_cparams(1),
        name="diff_sample",
    )(page_table, *lams, g.reshape(1, DIFF_V_DIM), z3, *tabs, k_new, v_new,
      *([cache_k] * n_pages), *([cache_v] * n_pages))


def _post_kernel(x_ref, mix_ref, mem_ref, woa_ref, wob_ref, g2_ref, w1_ref, w2_ref, *rest, final):
    if final:
        gf_ref, o_ref = rest
    else:
        (o_ref,) = rest
    o_ref[...] = (x_ref[...]
                  + jnp.dot(mix_ref[...], woa_ref[...], preferred_element_type=F32)
                  + jnp.dot(mem_ref[...], wob_ref[...], preferred_element_type=F32))
    h = (_rms_unit(o_ref[...]) * g2_ref[...]).astype(BF16)
    for c in range(D_FF // FF_CHUNK):
        sl = slice(c * FF_CHUNK, (c + 1) * FF_CHUNK)
        a = jnp.maximum(jnp.dot(h, w1_ref[:, sl], preferred_element_type=F32), 0.0)
        o_ref[...] += jnp.dot((a * a).astype(BF16), w2_ref[sl, :], preferred_element_type=F32)
    if final:
        o_ref[...] = _rms_unit(o_ref[...]) * gf_ref[...]


def _post(x, mix, mem, woa, wob, g2, w1, w2, gf, tm):
    m, d = x.shape
    final = gf is not None
    row = lambda n: pl.BlockSpec((tm, n), lambda i: (i, 0))
    in_specs = [row(d), row(MIX_WIDTH), row(MEM_WIDTH),
                _const_spec((MIX_WIDTH, d)), _const_spec((MEM_WIDTH, d)), _const_spec((1, d)),
                _const_spec((d, D_FF)), _const_spec((D_FF, d))]
    args = [x, mix, mem, woa, wob, g2.reshape(1, d), w1, w2]
    if final:
        in_specs.append(_const_spec((1, d)))
        args.append(gf.reshape(1, d))
    return pl.pallas_call(
        functools.partial(_post_kernel, final=final),
        grid=(m // tm,),
        in_specs=in_specs,
        out_specs=row(d),
        out_shape=jax.ShapeDtypeStruct((m, d), F32),
        compiler_params=_cparams(1),
        name="post_final" if final else "post",
    )(*args)


def _rope_tables(pos):
    half = DIFF_HEAD_DIM // 2
    inv = 1.0 / (ROPE_THETA ** (jnp.arange(0, DIFF_HEAD_DIM, 2, dtype=F32) / DIFF_HEAD_DIM))
    ang = pos.astype(F32)[:, None] * inv[None, :]
    cos, sin = jnp.cos(ang), jnp.sin(ang)
    zero = jnp.zeros_like(sin)
    assert cos.shape[1] == half
    tile2 = lambda a, b: jnp.concatenate([a, b, a, b], axis=-1)
    return tile2(cos, cos), tile2(-sin, zero), tile2(zero, sin)


def _lam_init(layer):
    return 0.8 - 0.6 * math.exp(-0.3 * layer)


def kernel(x_prompt, x_sample, state_pool, cache_mem_k, cache_mem_v, cache_k, cache_v, page_table, mem_prompt,
           norm1, norm2, final_norm, kv_norm, mem_norm, w_in, w_out, w_mem_k, w_mem_v, w_pool, pool_scale,
           w_k, w_v, lambda_q1, lambda_k1, lambda_q2, lambda_k2, subln, w_mlp1, w_mlp2):
    bp, tp, d = x_prompt.shape
    bs, ts, _ = x_sample.shape
    n_pages = page_table.shape[1]
    assert d == D_MODEL and n_pages * PAGE_SIZE == PAST_LEN

    w_in_b = w_in.astype(BF16)
    woa_b = w_out[:, :MIX_WIDTH].astype(BF16)
    wob_b = w_out[:, MIX_WIDTH:].astype(BF16)
    w1_b = w_mlp1.astype(BF16)
    w2_b = w_mlp2.astype(BF16)
    wk_b = w_k.astype(BF16)
    wv_b = w_v.astype(BF16)
    w_memkv = jnp.concatenate([w_mem_k, w_mem_v], axis=-1)
    w_memkv = w_memkv.transpose(1, 0, 2).reshape(d, DEPTH * 2 * MEM_WIDTH).astype(BF16)
    wbd = jnp.zeros((N_A, MIX_WIDTH, MIX_WIDTH), F32)
    for g in range(len(POOL_WINDOWS)):
        sl = slice(g * GROUP_WIDTH, (g + 1) * GROUP_WIDTH)
        wbd = wbd.at[:, sl, sl].set(w_pool[:, g])
    wbd = wbd.astype(BF16)
    lams = (lambda_q1, lambda_k1, lambda_q2, lambda_k2)

    tm = 512
    mp = bp * tp
    memkv = _norm_mm(mem_prompt.reshape(bp * MEM_TOKENS, d), mem_norm, w_memkv, 256)
    memkv5 = memkv.reshape(bp, MEM_TOKENS, DEPTH, 2, MEM_HEADS, MEM_HEAD_DIM)
    mem_k_prompt = memkv5[:, :, :, 0].transpose(2, 0, 1, 3, 4)
    mem_v_prompt = memkv5[:, :, :, 1].transpose(2, 0, 1, 3, 4)

    tabs_p = _rope_tables(jnp.arange(tp, dtype=jnp.int32))
    x = x_prompt.reshape(mp, d)
    pool_prompt = []
    z = _norm_mm(x, norm1[0], w_in_b[0], tm)
    for l in range(DEPTH):
        if l < N_A:
            pool_prompt.append(z.reshape(bp, tp, d)[:, tp - POOL_HIST:, :MIX_WIDTH])
            mix = _pool_prompt(z, wbd[l], pool_scale[l], tp, tm)
        else:
            j = l - N_A
            mix = _diff_prompt(z, kt_p, vb_p, tabs_p, lams, subln[j], j, _lam_init(l), bp, tp, tm)
        mem = _mem_prompt(z, memkv, l, tp, tm)
        gf = final_norm if l == DEPTH - 1 else None
        x = _post(x, mix, mem, woa_b[l], wob_b[l], norm2[l], w1_b[l], w2_b[l], gf, tm)
        if l + 1 == N_A:
            k_p, v_p, z, kt_p, vb_p = _kv_proj(x, kv_norm, norm1[l + 1], wk_b, wv_b, w_in_b[l + 1],
                                               tabs_p, tm, True)
        elif l + 1 < DEPTH:
            z = _norm_mm(x, norm1[l + 1], w_in_b[l + 1], tm)
    y_prompt = x.reshape(bp, tp, d)
    pool_prompt = jnp.stack(pool_prompt, axis=0)
    k_prompt = k_p.reshape(bp, tp, DIFF_HEADS, 2, DIFF_HEAD_DIM)
    v_prompt = v_p.reshape(bp, tp, DIFF_HEADS, DIFF_V_DIM)

    ms = bs * ts
    tms = 256
    pos_s = PAST_LEN + jnp.arange(ts, dtype=jnp.int32)
    tabs_s = _rope_tables(jnp.tile(pos_s, ms // ts))
    tabs_s2 = _rope_tables(jnp.tile(pos_s, 2))
    ck = cache_k.reshape(cache_k.shape[0], PAGE_SIZE, MIX_WIDTH)
    cv = cache_v.reshape(cache_v.shape[0], PAGE_SIZE, MIX_WIDTH)
    cmk = cache_mem_k.reshape(DEPTH, bs, MEM_TOKENS, MEM_WIDTH)
    cmv = cache_mem_v.reshape(DEPTH, bs, MEM_TOKENS, MEM_WIDTH)
    head_of_col = jnp.arange(MEM_WIDTH) // MEM_HEAD_DIM
    head_mask = (head_of_col[None, :] == jnp.arange(MEM_HEADS)[:, None]).astype(F32)
    ext_rows = POOL_PAD + 2 * ts
    assert ext_rows % 8 == 0

    x = x_sample.reshape(ms, d)
    pool_sample = []
    z = _norm_mm(x, norm1[0], w_in_b[0], tms)
    for l in range(DEPTH):
        z3 = z.reshape(bs, ts, d)
        if l < N_A:
            u = z3[:, :, :MIX_WIDTH]
            pool_sample.append(jnp.concatenate([state_pool[l], u], axis=1)[:, -POOL_HIST:])
            ext = jnp.concatenate([jnp.zeros((bs, 1, MIX_WIDTH), F32), state_pool[l], u,
                                   jnp.zeros((bs, ext_rows - POOL_PAD - ts, MIX_WIDTH), F32)], axis=1)
            mix = _pool_sample(ext.reshape(bs * ext_rows, MIX_WIDTH), wbd[l], pool_scale[l], 32 * ext_rows)
            mix = mix.reshape(bs, ext_rows, MIX_WIDTH)[:, POOL_PAD:POOL_PAD + ts].reshape(ms, MIX_WIDTH)
        else:
            j = l - N_A
            mix = _diff_sample(page_table, z3, tabs_s2, k_s.reshape(bs, ts, MIX_WIDTH),
                               v_s.reshape(bs, ts, MIX_WIDTH), ck, cv, lams, subln[j], j, _lam_init(l))
            mix = mix.reshape(ms, MIX_WIDTH)
        qm = z3[:, None, :, MIX_WIDTH:] * (MEM_HEAD_DIM ** -0.5) * head_mask[None, :, None, :]
        om = _mem_sample(qm.reshape(bs, MEM_HEADS * ts, MEM_WIDTH).astype(BF16), cmk, cmv, l, 16)
        mem = jnp.sum(om.reshape(bs, MEM_HEADS, ts, MEM_WIDTH) * head_mask[None, :, None, :], axis=1)
        mem = mem.reshape(ms, MEM_WIDTH).astype(BF16)
        gf = final_norm if l == DEPTH - 1 else None
        x = _post(x, mix, mem, woa_b[l], wob_b[l], norm2[l], w1_b[l], w2_b[l], gf, tms)
        if l + 1 == N_A:
            k_s, v_s, z = _kv_proj(x, kv_norm, norm1[l + 1], wk_b, wv_b, w_in_b[l + 1], tabs_s, tms, False)
        elif l + 1 < DEPTH:
            z = _norm_mm(x, norm1[l + 1], w_in_b[l + 1], tms)
    y_sample = x.reshape(bs, ts, d)
    pool_sample = jnp.stack(pool_sample, axis=0)
    k_sample = k_s.reshape(bs, ts, DIFF_HEADS, 2, DIFF_HEAD_DIM)
    v_sample = v_s.reshape(bs, ts, DIFF_HEADS, DIFF_V_DIM)

    return (y_prompt, y_sample, pool_prompt, mem_k_prompt, mem_v_prompt, k_prompt, v_prompt,
            pool_sample, k_sample, v_sample)
```

```python
import functools
import math

import jax
import jax.numpy as jnp
from jax import lax
from jax.experimental import pallas as pl
from jax.experimental.pallas import tpu as pltpu

D_MODEL = 1024
DEPTH = 4
N_A = DEPTH // 2
N_B = DEPTH - N_A
PAST_LEN = 2048
PAGE_SIZE = 128
MEM_TOKENS = 256
MEM_HEADS = 4
MEM_HEAD_DIM = D_MODEL // 16
MEM_WIDTH = MEM_HEADS * MEM_HEAD_DIM
MIX_WIDTH = D_MODEL - MEM_WIDTH
POOL_WINDOWS = (2, 4, 8, 16)
GROUP_WIDTH = MIX_WIDTH // len(POOL_WINDOWS)
POOL_HIST = max(POOL_WINDOWS) - 1
POOL_PAD = POOL_HIST + 1
DIFF_HEAD_DIM = 64
DIFF_HEADS = MIX_WIDTH // (2 * DIFF_HEAD_DIM)
DIFF_V_DIM = 2 * DIFF_HEAD_DIM
D_FF = 4 * D_MODEL
ROPE_THETA = 10000.0
RMS_EPS = 1e-6
NEG_INF = -1e30

LANES = 128
FF_CHUNK = 1024
VMEM_LIMIT = 56 * 1024 * 1024

BF16 = jnp.bfloat16
F32 = jnp.float32


def _cparams(n_axes):
    return pltpu.CompilerParams(dimension_semantics=("arbitrary",) * n_axes,
                                vmem_limit_bytes=VMEM_LIMIT)


def _const_spec(shape):
    nd = len(shape)
    return pl.BlockSpec(shape, lambda *_: (0,) * nd)


def _rms_unit(x):
    return x * lax.rsqrt(jnp.mean(x * x, axis=-1, keepdims=True) + RMS_EPS)


def _rope_slab(x, cos, sin_lo, sin_hi):
    return (x * cos + pltpu.roll(x, LANES - DIFF_HEAD_DIM // 2, 1) * sin_lo
            + pltpu.roll(x, DIFF_HEAD_DIM // 2, 1) * sin_hi)


def _lambda_value(lq1, lk1, lq2, lk2, j, lam_init):
    a = jnp.sum(lq1[j:j + 1, :] * lk1[j:j + 1, :], axis=-1, keepdims=True)
    b = jnp.sum(lq2[j:j + 1, :] * lk2[j:j + 1, :], axis=-1, keepdims=True)
    return jnp.exp(a) - jnp.exp(b) + lam_init


def _group_select(lane, s2, s4, s8, s16):
    return jnp.where(lane < GROUP_WIDTH, s2 * (1.0 / POOL_WINDOWS[0]),
                     jnp.where(lane < 2 * GROUP_WIDTH, s4 * (1.0 / POOL_WINDOWS[1]),
                               jnp.where(lane < 3 * GROUP_WIDTH, s8 * (1.0 / POOL_WINDOWS[2]),
                                         s16 * (1.0 / POOL_WINDOWS[3]))))


def _norm_mm_kernel(x_ref, g_ref, w_ref, o_ref):
    h = _rms_unit(x_ref[...]) * g_ref[...]
    o_ref[...] = jnp.dot(h.astype(BF16), w_ref[...], preferred_element_type=F32)


def _norm_mm(x, g, w, tm):
    m, d = x.shape
    n = w.shape[1]
    return pl.pallas_call(
        _norm_mm_kernel,
        grid=(m // tm,),
        in_specs=[pl.BlockSpec((tm, d), lambda i: (i, 0)),
                  _const_spec((1, d)),
                  _const_spec((d, n))],
        out_specs=pl.BlockSpec((tm, n), lambda i: (i, 0)),
        out_shape=jax.ShapeDtypeStruct((m, n), F32),
        compiler_params=_cparams(1),
        name="norm_mm",
    )(x, g.reshape(1, d), w)


def _mem_proj_kernel(x_ref, g_ref, w_ref, kt_ref, vt_ref, ktb_ref, vb_ref):
    h = (_rms_unit(x_ref[...]) * g_ref[...]).astype(BF16)
    y = jnp.dot(h, w_ref[...], preferred_element_type=F32)
    for l in range(DEPTH):
        yk = y[:, (2 * l) * MEM_WIDTH:(2 * l + 1) * MEM_WIDTH]
        yv = y[:, (2 * l + 1) * MEM_WIDTH:(2 * l + 2) * MEM_WIDTH]
        ykt = yk.T
        kt_ref[l, 0] = ykt
        ktb_ref[l, 0] = ykt.astype(BF16)
        vt_ref[l, 0] = yv.T
        vb_ref[l, 0] = yv.astype(BF16)


def _mem_proj(x, g, w, batch):
    d = x.shape[1]
    spec = pl.BlockSpec((DEPTH, 1, MEM_WIDTH, MEM_TOKENS), lambda b: (0, b, 0, 0))
    spec_n = pl.BlockSpec((DEPTH, 1, MEM_TOKENS, MEM_WIDTH), lambda b: (0, b, 0, 0))
    return pl.pallas_call(
        _mem_proj_kernel,
        grid=(batch,),
        in_specs=[pl.BlockSpec((MEM_TOKENS, d), lambda b: (b, 0)),
                  _const_spec((1, d)),
                  _const_spec((d, w.shape[1]))],
        out_specs=[spec, spec, spec, spec_n],
        out_shape=[jax.ShapeDtypeStruct((DEPTH, batch, MEM_WIDTH, MEM_TOKENS), F32),
                   jax.ShapeDtypeStruct((DEPTH, batch, MEM_WIDTH, MEM_TOKENS), F32),
                   jax.ShapeDtypeStruct((DEPTH, batch, MEM_WIDTH, MEM_TOKENS), BF16),
                   jax.ShapeDtypeStruct((DEPTH, batch, MEM_TOKENS, MEM_WIDTH), BF16)],
        compiler_params=_cparams(1),
        name="mem_proj",
    )(x, g.reshape(1, d), w)


def _kv_kernel(x_ref, gkv_ref, gn_ref, wk_ref, wv_ref, win_ref, cos_ref, slo_ref, shi_ref,
               kt_ref, vh_ref, z_ref, a_ref, b_ref, *, attention_copies):
    y = _rms_unit(x_ref[...])
    hk = (y * gkv_ref[...]).astype(BF16)
    hn = (y * gn_ref[...]).astype(BF16)
    z_ref[...] = jnp.dot(hn, win_ref[...], preferred_element_type=F32)
    v = jnp.dot(hk, wv_ref[...], preferred_element_type=F32)
    kraw = jnp.dot(hk, wk_ref[...], preferred_element_type=F32)
    cos, slo, shi = cos_ref[...], slo_ref[...], shi_ref[...]
    if not attention_copies:
        b_ref[...] = v
    for h in range(DIFF_HEADS):
        sl = slice(h * LANES, (h + 1) * LANES)
        kr = _rope_slab(kraw[:, sl], cos, slo, shi)
        krt = kr.T
        kt_ref[0, sl, :] = krt
        vh_ref[0, h] = v[:, sl]
        if attention_copies:
            a_ref[0, sl, :] = krt.astype(BF16)
            b_ref[0, h] = v[:, sl].astype(BF16)
        else:
            a_ref[:, sl] = kr


def _kv_proj(x, gkv, gn, wk, wv, win, tabs, n_seq, seq, tm, attention_copies):
    m, d = x.shape
    tps = seq // tm
    n_tab = tabs[0].shape[0] // tm
    tab_spec = pl.BlockSpec((tm, LANES), lambda i: (i % n_tab, 0))
    row = lambda n: pl.BlockSpec((tm, n), lambda i: (i, 0))
    kt_spec = pl.BlockSpec((1, MIX_WIDTH, tm), lambda i: (i // tps, 0, i % tps))
    vh_spec = pl.BlockSpec((1, DIFF_HEADS, tm, LANES), lambda i: (i // tps, 0, i % tps, 0))
    out_specs = [kt_spec, vh_spec, row(D_MODEL)]
    out_shape = [jax.ShapeDtypeStruct((n_seq, MIX_WIDTH, seq), F32),
                 jax.ShapeDtypeStruct((n_seq, DIFF_HEADS, seq, LANES), F32),
                 jax.ShapeDtypeStruct((m, D_MODEL), F32)]
    if attention_copies:
        out_specs += [pl.BlockSpec((1, MIX_WIDTH, tm), lambda i: (i, 0, 0)), vh_spec]
        out_shape += [jax.ShapeDtypeStruct((m // tm, MIX_WIDTH, tm), BF16),
                      jax.ShapeDtypeStruct((n_seq, DIFF_HEADS, seq, LANES), BF16)]
    else:
        out_specs += [row(MIX_WIDTH), row(MIX_WIDTH)]
        out_shape += [jax.ShapeDtypeStruct((m, MIX_WIDTH), F32)] * 2
    return pl.pallas_call(
        functools.partial(_kv_kernel, attention_copies=attention_copies),
        grid=(m // tm,),
        in_specs=[row(d), _const_spec((1, d)), _const_spec((1, d)),
                  _const_spec((d, MIX_WIDTH)), _const_spec((d, MIX_WIDTH)), _const_spec((d, D_MODEL)),
                  tab_spec, tab_spec, tab_spec],
        out_specs=out_specs,
        out_shape=out_shape,
        compiler_params=_cparams(1),
        name="kv_proj",
    )(x, gkv.reshape(1, d), gn.reshape(1, d), wk, wv, win, *tabs)


def _pool_prompt_kernel(u_ref, h_ref, w_ref, scale_ref, o_ref, *, tiles_per_seq, tm):
    t0 = (pl.program_id(0) % tiles_per_seq) * tm
    hist = jnp.where(t0 > 0, h_ref[...], 0.0)
    ext = jnp.concatenate([hist, u_ref[...]], axis=0)
    s2 = ext + pltpu.roll(ext, 1, 0)
    s4 = s2 + pltpu.roll(s2, 2, 0)
    s8 = s4 + pltpu.roll(s4, 4, 0)
    s16 = s8 + pltpu.roll(s8, 8, 0)
    pos = t0 - POOL_PAD + lax.broadcasted_iota(jnp.int32, (tm + POOL_PAD, 1), 0)
    inv = [1.0 / jnp.clip(pos + 1, 1, w).astype(F32) for w in POOL_WINDOWS]
    lane = lax.broadcasted_iota(jnp.int32, (1, MIX_WIDTH), 1)
    mean = jnp.where(lane < GROUP_WIDTH, s2 * inv[0],
                     jnp.where(lane < 2 * GROUP_WIDTH, s4 * inv[1],
                               jnp.where(lane < 3 * GROUP_WIDTH, s8 * inv[2], s16 * inv[3])))
    pooled = (mean - ext)[POOL_PAD:]
    mix = jnp.dot(pooled.astype(BF16), w_ref[...], preferred_element_type=F32) * scale_ref[...]
    o_ref[...] = mix.astype(o_ref.dtype)


def _pool_prompt(z, wbd, scale, seq, tm):
    m = z.shape[0]
    hist_blocks = tm // POOL_PAD
    return pl.pallas_call(
        functools.partial(_pool_prompt_kernel, tiles_per_seq=seq // tm, tm=tm),
        grid=(m // tm,),
        in_specs=[pl.BlockSpec((tm, MIX_WIDTH), lambda i: (i, 0)),
                  pl.BlockSpec((POOL_PAD, MIX_WIDTH), lambda i: (jnp.maximum(i * hist_blocks - 1, 0), 0)),
                  _const_spec((MIX_WIDTH, MIX_WIDTH)),
                  _const_spec((1, MIX_WIDTH))],
        out_specs=pl.BlockSpec((tm, MIX_WIDTH), lambda i: (i, 0)),
        out_shape=jax.ShapeDtypeStruct((m, MIX_WIDTH), BF16),
        compiler_params=_cparams(1),
        name="pool_prompt",
    )(z, z, wbd, scale.reshape(1, MIX_WIDTH))


def _pool_sample_kernel(hist_ref, u_ref, w_ref, scale_ref, mix_ref, hist_out_ref, *, n_tok):
    rows = [hist_ref[0, t] for t in range(POOL_HIST)] + [u_ref[t] for t in range(n_tok)]
    lane = lax.broadcasted_iota(jnp.int32, (1, MIX_WIDTH), 1)
    pooled = []
    for t in range(n_tok):
        c = POOL_HIST + t
        back = lambda a, b: functools.reduce(lambda x, y: x + y, [rows[c - i] for i in range(a, b)])
        s2 = back(0, 2)
        s4 = s2 + back(2, 4)
        s8 = s4 + back(4, 8)
        s16 = s8 + back(8, 16)
        pooled.append((_group_select(lane, s2, s4, s8, s16) - rows[c]).astype(BF16))
    nb = pooled[0].shape[0]
    mix = jnp.dot(jnp.concatenate(pooled, axis=0), w_ref[...], preferred_element_type=F32) * scale_ref[...]
    for t in range(n_tok):
        mix_ref[t] = mix[t * nb:(t + 1) * nb].astype(mix_ref.dtype)
    for t in range(POOL_HIST):
        hist_out_ref[t] = rows[t + n_tok]


def _pool_sample(hist, z3, layer, wbd, scale, bb):
    n_tok, nb, _ = z3.shape
    return pl.pallas_call(
        functools.partial(_pool_sample_kernel, n_tok=n_tok),
        grid=(nb // bb,),
        in_specs=[pl.BlockSpec((1, POOL_HIST, bb, MIX_WIDTH), lambda i: (layer, 0, i, 0)),
                  pl.BlockSpec((n_tok, bb, MIX_WIDTH), lambda i: (0, i, 0)),
                  _const_spec((MIX_WIDTH, MIX_WIDTH)),
                  _const_spec((1, MIX_WIDTH))],
        out_specs=[pl.BlockSpec((n_tok, bb, MIX_WIDTH), lambda i: (0, i, 0)),
                   pl.BlockSpec((POOL_HIST, bb, MIX_WIDTH), lambda i: (0, i, 0))],
        out_shape=[jax.ShapeDtypeStruct((n_tok, nb, MIX_WIDTH), BF16),
                   jax.ShapeDtypeStruct((POOL_HIST, nb, MIX_WIDTH), F32)],
        compiler_params=_cparams(1),
        name="pool_sample",
    )(hist, z3, wbd, scale.reshape(1, MIX_WIDTH))


def _softmax_rows(s):
    e = jnp.exp(s - jnp.max(s, axis=-1, keepdims=True))
    return e * (1.0 / jnp.sum(e, axis=-1, keepdims=True))


def _mem_prompt_kernel(q_ref, mkt_ref, mv_ref, o_ref):
    q = q_ref[...] * (MEM_HEAD_DIM ** -0.5)
    mkt = mkt_ref[0, 0]
    mv = mv_ref[0, 0]
    lane = lax.broadcasted_iota(jnp.int32, (1, MEM_WIDTH), 1)
    ps, mvs = [], []
    for h in range(MEM_HEADS):
        head = (lane >= h * MEM_HEAD_DIM) & (lane < (h + 1) * MEM_HEAD_DIM)
        qh = jnp.where(head, q, 0.0).astype(BF16)
        s = jnp.dot(qh, mkt, preferred_element_type=F32)
        ps.append(_softmax_rows(s).astype(BF16))
        mvs.append(jnp.where(head, mv, jnp.zeros_like(mv)))
    o = jnp.dot(jnp.concatenate(ps, axis=1), jnp.concatenate(mvs, axis=0), preferred_element_type=F32)
    o_ref[...] = o.astype(o_ref.dtype)


def _mem_prompt(z, mkt, mv, layer, seq, tq):
    m = z.shape[0]
    tiles_per_seq = seq // tq
    q_col = MIX_WIDTH // MEM_WIDTH
    kv_spec = pl.BlockSpec((1, 1, MEM_WIDTH, MEM_TOKENS), lambda i: (layer, i // tiles_per_seq, 0, 0))
    return pl.pallas_call(
        _mem_prompt_kernel,
        grid=(m // tq,),
        in_specs=[pl.BlockSpec((tq, MEM_WIDTH), lambda i: (i, q_col)), kv_spec, kv_spec],
        out_specs=pl.BlockSpec((tq, MEM_WIDTH), lambda i: (i, 0)),
        out_shape=jax.ShapeDtypeStruct((m, MEM_WIDTH), BF16),
        compiler_params=_cparams(1),
        name="mem_prompt",
    )(z, mkt, mv)


def _mem_sample_kernel(q_ref, mkt_ref, mvt_ref, o_ref):
    q = q_ref[...]
    mkt = mkt_ref[0].astype(BF16)
    mvt = mvt_ref[0].astype(BF16)
    s = jnp.einsum('bqd,bdm->bqm', q, mkt, preferred_element_type=F32)
    p = _softmax_rows(s).astype(BF16)
    o_ref[...] = jnp.einsum('bqm,bdm->bqd', p, mvt, preferred_element_type=F32)


def _mem_sample(q_rows, cache_kt, cache_vt, layer, bb):
    nb, rows, _ = q_rows.shape
    kv_spec = pl.BlockSpec((1, bb, MEM_WIDTH, MEM_TOKENS), lambda i: (layer, i, 0, 0))
    return pl.pallas_call(
        _mem_sample_kernel,
        grid=(nb // bb,),
        in_specs=[pl.BlockSpec((bb, rows, MEM_WIDTH), lambda i: (i, 0, 0)), kv_spec, kv_spec],
        out_specs=pl.BlockSpec((bb, rows, MEM_WIDTH), lambda i: (i, 0, 0)),
        out_shape=jax.ShapeDtypeStruct((nb, rows, MEM_WIDTH), F32),
        compiler_params=_cparams(1),
        name="mem_sample",
    )(q_rows, cache_kt, cache_vt)


def _subln(o, g_ref, lam_init):
    return _rms_unit(o) * g_ref[...] * (1.0 - lam_init)


def _lane_tiles(x):
    return [x[:, c * LANES:(c + 1) * LANES] for c in range(x.shape[1] // LANES)]


def _diff_prompt_kernel(lq1, lk1, lq2, lk2, g_ref, q_ref, cos_ref, slo_ref, shi_ref, kt_ref, v_ref,
                        o_ref, qm_ref, s_ref, m_ref, l_ref, acc_ref, *, j, lam_init, tq):
    qi = pl.program_id(2)
    q = _rope_slab(q_ref[...], cos_ref[...], slo_ref[...], shi_ref[...]) * (DIFF_HEAD_DIM ** -0.5)
    lane = lax.broadcasted_iota(jnp.int32, (1, LANES), 1)
    qm_ref[0] = jnp.where(lane < DIFF_HEAD_DIM, q, 0.0).astype(BF16)
    qm_ref[1] = jnp.where(lane >= DIFF_HEAD_DIM, q, 0.0).astype(BF16)
    m_ref[...] = jnp.full(m_ref.shape, NEG_INF, F32)

    def scores(kv, masked):
        kt = kt_ref[0, kv]
        for a in range(2):
            s = jnp.dot(qm_ref[a], kt, preferred_element_type=F32)
            if masked:
                row = lax.broadcasted_iota(jnp.int32, (tq, tq), 0)
                col = lax.broadcasted_iota(jnp.int32, (tq, tq), 1)
                s = jnp.where(col <= row, s, NEG_INF)
            s_ref[a, kv] = s
            m_ref[a] = functools.reduce(jnp.maximum, _lane_tiles(s), m_ref[a])

    def scores_body(kv, carry):
        scores(kv, False)
        return carry

    lax.fori_loop(0, qi, scores_body, 0)
    scores(qi, True)
    for a in range(2):
        m_ref[a] = jnp.broadcast_to(jnp.max(m_ref[a], axis=-1, keepdims=True), (tq, LANES))
    l_ref[...] = jnp.zeros(l_ref.shape, F32)
    acc_ref[...] = jnp.zeros(acc_ref.shape, F32)

    def weigh(kv, carry):
        v = v_ref[0, 0, pl.ds(pl.multiple_of(kv * tq, tq), tq), :]
        for a in range(2):
            m = m_ref[a]
            p = [jnp.exp(t - m) for t in _lane_tiles(s_ref[a, kv])]
            l_ref[a] += functools.reduce(lambda x, y: x + y, p)
            acc_ref[a] += jnp.dot(jnp.concatenate(p, axis=1).astype(BF16), v, preferred_element_type=F32)
        return carry

    lax.fori_loop(0, qi + 1, weigh, 0)

    lam = _lambda_value(lq1[...], lk1[...], lq2[...], lk2[...], j, lam_init)
    inv = [1.0 / jnp.sum(l_ref[a], axis=-1, keepdims=True) for a in range(2)]
    o = acc_ref[0] * inv[0] - lam * (acc_ref[1] * inv[1])
    o_ref[...] = _subln(o, g_ref, lam_init).astype(o_ref.dtype)


def _diff_prompt(z, kt, vb, tabs, lams, g, j, lam_init, batch, seq, tq):
    m = z.shape[0]
    nq = seq // tq
    lam_spec = _const_spec((N_B, DIFF_HEAD_DIM))
    tab_spec = pl.BlockSpec((tq, LANES), lambda b, h, i: (i, 0))
    return pl.pallas_call(
        functools.partial(_diff_prompt_kernel, j=j, lam_init=lam_init, tq=tq),
        grid=(batch, DIFF_HEADS, nq),
        in_specs=[lam_spec, lam_spec, lam_spec, lam_spec,
                  _const_spec((1, DIFF_V_DIM)),
                  pl.BlockSpec((tq, LANES), lambda b, h, i: (b * nq + i, h)),
                  tab_spec, tab_spec, tab_spec,
                  pl.BlockSpec((1, nq, LANES, tq), lambda b, h, i: (b, 0, h, 0)),
                  pl.BlockSpec((1, 1, seq, LANES), lambda b, h, i: (b, h, 0, 0))],
        out_specs=pl.BlockSpec((tq, LANES), lambda b, h, i: (b * nq + i, h)),
        out_shape=jax.ShapeDtypeStruct((m, MIX_WIDTH), BF16),
        scratch_shapes=[pltpu.VMEM((2, tq, LANES), BF16),
                        pltpu.VMEM((2, nq, tq, tq), F32),
                        pltpu.VMEM((2, tq, LANES), F32),
                        pltpu.VMEM((2, tq, LANES), F32),
                        pltpu.VMEM((2, tq, LANES), F32)],
        compiler_params=_cparams(3),
        name="diff_prompt",
    )(*lams, g.reshape(1, DIFF_V_DIM), z, *tabs, kt.reshape(batch, nq, MIX_WIDTH, tq), vb)


def _diff_sample_kernel(pt_ref, lq1, lk1, lq2, lk2, g_ref, z_ref, cos_ref, slo_ref, shi_ref,
                        kn_ref, vn_ref, *rest, j, lam_init, n_pages, n_tok):
    del pt_ref
    k_pages = rest[:n_pages]
    v_pages = rest[n_pages:2 * n_pages]
    o_ref, x_scr, pg_scr, kb, vb = rest[2 * n_pages:]
    past = n_pages * PAGE_SIZE
    rows = 2 * n_tok

    x_scr[...] = jnp.zeros(x_scr.shape, F32)
    x_scr[0:n_tok, :] = z_ref[0]
    x = x_scr[...]
    x = x + pltpu.roll(x, n_tok, 0)

    for r in range(n_pages):
        cols = slice(r * PAGE_SIZE, (r + 1) * PAGE_SIZE)
        for h in range(DIFF_HEADS):
            kb[h, :, cols] = k_pages[r][0, h * LANES:(h + 1) * LANES, :].astype(BF16)
            vb[h, cols, :] = v_pages[r][0, h].astype(BF16)
    new = slice(past, past + PAGE_SIZE)
    pg_scr[...] = jnp.zeros(pg_scr.shape, F32)
    pg_scr[0:n_tok, :] = kn_ref[0]
    for h in range(DIFF_HEADS):
        kb[h, :, new] = pg_scr[:, h * LANES:(h + 1) * LANES].T.astype(BF16)
    pg_scr[0:n_tok, :] = vn_ref[0]
    for h in range(DIFF_HEADS):
        vb[h, new, :] = pg_scr[:, h * LANES:(h + 1) * LANES].astype(BF16)

    lam = _lambda_value(lq1[...], lk1[...], lq2[...], lk2[...], j, lam_init)
    row = lax.broadcasted_iota(jnp.int32, (rows, LANES), 0)
    lane = lax.broadcasted_iota(jnp.int32, (rows, LANES), 1)
    own_map = (row < n_tok) == (lane < DIFF_HEAD_DIM)
    n_keys = past + PAGE_SIZE
    key = lax.broadcasted_iota(jnp.int32, (rows, n_keys), 1)
    tok = lax.broadcasted_iota(jnp.int32, (rows, n_keys), 0) % n_tok
    visible = key - past <= tok
    cos, slo, shi = cos_ref[...], slo_ref[...], shi_ref[...]

    for h in range(DIFF_HEADS):
        sl = slice(h * LANES, (h + 1) * LANES)
        q = _rope_slab(x[:, sl], cos, slo, shi) * (DIFF_HEAD_DIM ** -0.5)
        q = jnp.where(own_map, q, 0.0).astype(BF16)
        s = jnp.dot(q, kb[h], preferred_element_type=F32)
        p = _softmax_rows(jnp.where(visible, s, NEG_INF))
        a = p - lam * pltpu.roll(p, n_tok, 0)
        o = jnp.dot(a.astype(BF16), vb[h], preferred_element_type=F32)
        o_ref[0, :, sl] = _subln(o[0:n_tok], g_ref, lam_init).astype(o_ref.dtype)


def _diff_sample(page_table, zb, tabs, k_new, v_new, cache_kt, cache_vh, lams, g, j, lam_init):
    nb, n_tok, _ = zb.shape
    n_pages = page_table.shape[1]
    rows = 2 * n_tok
    n_keys = (n_pages + 1) * PAGE_SIZE
    lam_spec = _const_spec((N_B, DIFF_HEAD_DIM))
    tab_spec = _const_spec((rows, LANES))
    tok_spec = pl.BlockSpec((1, n_tok, MIX_WIDTH), lambda b, pt: (b, 0, 0))

    def k_spec(r):
        return pl.BlockSpec((1, MIX_WIDTH, PAGE_SIZE), lambda b, pt: (pt[b, r], 0, 0))

    def v_spec(r):
        return pl.BlockSpec((1, DIFF_HEADS, PAGE_SIZE, LANES), lambda b, pt: (pt[b, r], 0, 0, 0))

    grid_spec = pltpu.PrefetchScalarGridSpec(
        num_scalar_prefetch=1,
        grid=(nb,),
        in_specs=[lam_spec, lam_spec, lam_spec, lam_spec, _const_spec((1, DIFF_V_DIM)),
                  tok_spec, tab_spec, tab_spec, tab_spec, tok_spec, tok_spec]
                 + [k_spec(r) for r in range(n_pages)] + [v_spec(r) for r in range(n_pages)],
        out_specs=tok_spec,
        scratch_shapes=[pltpu.VMEM((rows, MIX_WIDTH), F32),
                        pltpu.VMEM((PAGE_SIZE, MIX_WIDTH), F32),
                        pltpu.VMEM((DIFF_HEADS, LANES, n_keys), BF16),
                        pltpu.VMEM((DIFF_HEADS, n_keys, LANES), BF16)],
    )
    return pl.pallas_call(
        functools.partial(_diff_sample_kernel, j=j, lam_init=lam_init, n_pages=n_pages, n_tok=n_tok),
        grid_spec=grid_spec,
        out_shape=jax.ShapeDtypeStruct((nb, n_tok, MIX_WIDTH), BF16),
        compiler_params=_cparams(1),
        name="diff_sample",
    )(page_table, *lams, g.reshape(1, DIFF_V_DIM), zb, *tabs, k_new, v_new,
      *([cache_kt] * n_pages), *([cache_vh] * n_pages))


def _post_kernel(x_ref, mix_ref, mem_ref, woa_ref, wob_ref, g2_ref, w1_ref, w2_ref, *rest, final):
    if final:
        gf_ref, o_ref = rest
    else:
        (o_ref,) = rest
    o_ref[...] = (x_ref[...]
                  + jnp.dot(mix_ref[...], woa_ref[...], preferred_element_type=F32)
                  + jnp.dot(mem_ref[...], wob_ref[...], preferred_element_type=F32))
    h = (_rms_unit(o_ref[...]) * g2_ref[...]).astype(BF16)
    for c in range(D_FF // FF_CHUNK):
        sl = slice(c * FF_CHUNK, (c + 1) * FF_CHUNK)
        a = jnp.maximum(jnp.dot(h, w1_ref[:, sl], preferred_element_type=F32), 0.0)
        o_ref[...] += jnp.dot((a * a).astype(BF16), w2_ref[sl, :], preferred_element_type=F32)
    if final:
        o_ref[...] = _rms_unit(o_ref[...]) * gf_ref[...]


def _post(x, mix, mem, woa, wob, g2, w1, w2, gf, tm):
    m, d = x.shape
    final = gf is not None
    row = lambda n: pl.BlockSpec((tm, n), lambda i: (i, 0))
    in_specs = [row(d), row(MIX_WIDTH), row(MEM_WIDTH),
                _const_spec((MIX_WIDTH, d)), _const_spec((MEM_WIDTH, d)), _const_spec((1, d)),
                _const_spec((d, D_FF)), _const_spec((D_FF, d))]
    args = [x, mix, mem, woa, wob, g2.reshape(1, d), w1, w2]
    if final:
        in_specs.append(_const_spec((1, d)))
        args.append(gf.reshape(1, d))
    return pl.pallas_call(
        functools.partial(_post_kernel, final=final),
        grid=(m // tm,),
        in_specs=in_specs,
        out_specs=row(d),
        out_shape=jax.ShapeDtypeStruct((m, d), F32),
        compiler_params=_cparams(1),
        name="post_final" if final else "post",
    )(*args)


def _rope_tables(pos):
    inv = 1.0 / (ROPE_THETA ** (jnp.arange(0, DIFF_HEAD_DIM, 2, dtype=F32) / DIFF_HEAD_DIM))
    ang = pos.astype(F32)[:, None] * inv[None, :]
    cos, sin = jnp.cos(ang), jnp.sin(ang)
    zero = jnp.zeros_like(sin)
    tile2 = lambda a, b: jnp.concatenate([a, b, a, b], axis=-1)
    return tile2(cos, cos), tile2(-sin, zero), tile2(zero, sin)


def _lam_init(layer):
    return 0.8 - 0.6 * math.exp(-0.3 * layer)


def kernel(x_prompt, x_sample, state_pool, cache_mem_k, cache_mem_v, cache_k, cache_v, page_table, mem_prompt,
           norm1, norm2, final_norm, kv_norm, mem_norm, w_in, w_out, w_mem_k, w_mem_v, w_pool, pool_scale,
           w_k, w_v, lambda_q1, lambda_k1, lambda_q2, lambda_k2, subln, w_mlp1, w_mlp2):
    bp, tp, d = x_prompt.shape
    bs, ts, _ = x_sample.shape
    n_pages = page_table.shape[1]
    assert d == D_MODEL and n_pages * PAGE_SIZE == PAST_LEN and PAST_LEN >= POOL_HIST

    w_in_b = w_in.astype(BF16)
    woa_b = w_out[:, :MIX_WIDTH].astype(BF16)
    wob_b = w_out[:, MIX_WIDTH:].astype(BF16)
    w1_b = w_mlp1.astype(BF16)
    w2_b = w_mlp2.astype(BF16)
    wk_b = w_k.astype(BF16)
    wv_b = w_v.astype(BF16)
    w_memkv = jnp.concatenate([w_mem_k, w_mem_v], axis=-1)
    w_memkv = w_memkv.transpose(1, 0, 2).reshape(d, DEPTH * 2 * MEM_WIDTH).astype(BF16)
    wbd = jnp.zeros((N_A, MIX_WIDTH, MIX_WIDTH), F32)
    for g in range(len(POOL_WINDOWS)):
        sl = slice(g * GROUP_WIDTH, (g + 1) * GROUP_WIDTH)
        wbd = wbd.at[:, sl, sl].set(w_pool[:, g])
    wbd = wbd.astype(BF16)
    lams = (lambda_q1, lambda_k1, lambda_q2, lambda_k2)

    tm = 512
    mp = bp * tp
    mem_kt, mem_vt, mem_kt_b, mem_v_b = _mem_proj(mem_prompt.reshape(bp * MEM_TOKENS, d), mem_norm, w_memkv, bp)
    to_heads = lambda a: a.reshape(DEPTH, bp, MEM_HEADS, MEM_HEAD_DIM, MEM_TOKENS).transpose(0, 1, 4, 2, 3)
    mem_k_prompt, mem_v_prompt = to_heads(mem_kt), to_heads(mem_vt)

    tabs_p = _rope_tables(jnp.arange(tp, dtype=jnp.int32))
    x = x_prompt.reshape(mp, d)
    pool_prompt = []
    z = _norm_mm(x, norm1[0], w_in_b[0], tm)
    for l in range(DEPTH):
        if l < N_A:
            pool_prompt.append(z.reshape(bp, tp, d)[:, tp - POOL_HIST:, :MIX_WIDTH])
            mix = _pool_prompt(z, wbd[l], pool_scale[l], tp, tm)
        else:
            j = l - N_A
            mix = _diff_prompt(z, kt_b, vh_b, tabs_p, lams, subln[j], j, _lam_init(l), bp, tp, tm)
        mem = _mem_prompt(z, mem_kt_b, mem_v_b, l, tp, tm)
        gf = final_norm if l == DEPTH - 1 else None
        x = _post(x, mix, mem, woa_b[l], wob_b[l], norm2[l], w1_b[l], w2_b[l], gf, tm)
        if l + 1 == N_A:
            kt_p, vh_p, z, kt_b, vh_b = _kv_proj(x, kv_norm, norm1[l + 1], wk_b, wv_b, w_in_b[l + 1],
                                                 tabs_p, bp, tp, tm, True)
        elif l + 1 < DEPTH:
            z = _norm_mm(x, norm1[l + 1], w_in_b[l + 1], tm)
    y_prompt = x.reshape(bp, tp, d)
    pool_prompt = jnp.stack(pool_prompt, axis=0)
    k_prompt = kt_p.reshape(bp, DIFF_HEADS, 2, DIFF_HEAD_DIM, tp).transpose(0, 4, 1, 2, 3)
    v_prompt = vh_p.transpose(0, 2, 1, 3)

    ms = bs * ts
    pos_s = PAST_LEN + jnp.arange(ts, dtype=jnp.int32)
    tabs_s = _rope_tables(jnp.repeat(pos_s, bs))
    tabs_s2 = _rope_tables(jnp.tile(pos_s, 2))
    ck_t = cache_k.transpose(0, 2, 3, 4, 1).reshape(cache_k.shape[0], MIX_WIDTH, PAGE_SIZE)
    cv_h = cache_v.transpose(0, 2, 1, 3)
    cmk_t = cache_mem_k.transpose(0, 1, 3, 4, 2).reshape(DEPTH, bs, MEM_WIDTH, MEM_TOKENS)
    cmv_t = cache_mem_v.transpose(0, 1, 3, 4, 2).reshape(DEPTH, bs, MEM_WIDTH, MEM_TOKENS)
    hist = state_pool.transpose(0, 2, 1, 3)
    head_of_col = jnp.arange(MEM_WIDTH) // MEM_HEAD_DIM
    head_mask = (head_of_col[None, :] == jnp.arange(MEM_HEADS)[:, None]).astype(F32)
    to_requests = lambda a: a.reshape(ts, bs, -1).transpose(1, 0, 2)

    x = x_sample.transpose(1, 0, 2).reshape(ms, d)
    pool_sample = []
    z = _norm_mm(x, norm1[0], w_in_b[0], ms)
    for l in range(DEPTH):
        z3 = z.reshape(ts, bs, d)
        if l < N_A:
            mix, kept = _pool_sample(hist, z3, l, wbd[l], pool_scale[l], 32)
            pool_sample.append(kept)
            mix = mix.reshape(ms, MIX_WIDTH)
        else:
            j = l - N_A
            mix = _diff_sample(page_table, to_requests(z[:, :MIX_WIDTH]), tabs_s2, to_requests(k_s),
                               to_requests(v_s), ck_t, cv_h, lams, subln[j], j, _lam_init(l))
            mix = mix.transpose(1, 0, 2).reshape(ms, MIX_WIDTH)
        qm = to_requests(z[:, MIX_WIDTH:])[:, None] * (MEM_HEAD_DIM ** -0.5) * head_mask[None, :, None, :]
        om = _mem_sample(qm.reshape(bs, MEM_HEADS * ts, MEM_WIDTH).astype(BF16), cmk_t, cmv_t, l, 16)
        mem = jnp.sum(om.reshape(bs, MEM_HEADS, ts, MEM_WIDTH) * head_mask[None, :, None, :], axis=1)
        mem = mem.transpose(1, 0, 2).reshape(ms, MEM_WIDTH).astype(BF16)
        gf = final_norm if l == DEPTH - 1 else None
        x = _post(x, mix, mem, woa_b[l], wob_b[l], norm2[l], w1_b[l], w2_b[l], gf, ms // 2)
        if l + 1 == N_A:
            kt_s, vh_s, z, k_s, v_s = _kv_proj(x, kv_norm, norm1[l + 1], wk_b, wv_b, w_in_b[l + 1],
                                               tabs_s, ts, bs, bs, False)
        elif l + 1 < DEPTH:
            z = _norm_mm(x, norm1[l + 1], w_in_b[l + 1], ms)
    y_sample = x.reshape(ts, bs, d).transpose(1, 0, 2)
    pool_sample = jnp.stack(pool_sample, axis=0).transpose(0, 2, 1, 3)
    k_sample = kt_s.reshape(ts, DIFF_HEADS, 2, DIFF_HEAD_DIM, bs).transpose(4, 0, 1, 2, 3)
    v_sample = vh_s.transpose(2, 0, 1, 3)

    return (y_prompt, y_sample, pool_prompt, mem_k_prompt, mem_v_prompt, k_prompt, v_prompt,
            pool_sample, k_sample, v_sample)
```

```python
import functools
import math

import jax
import jax.numpy as jnp
from jax import lax
from jax.experimental import pallas as pl
from jax.experimental.pallas import tpu as pltpu

D_MODEL = 1024
DEPTH = 4
N_A = DEPTH // 2
N_B = DEPTH - N_A
PAST_LEN = 2048
PAGE_SIZE = 128
MEM_TOKENS = 256
MEM_HEADS = 4
MEM_HEAD_DIM = D_MODEL // 16
MEM_WIDTH = MEM_HEADS * MEM_HEAD_DIM
MIX_WIDTH = D_MODEL - MEM_WIDTH
POOL_WINDOWS = (2, 4, 8, 16)
GROUP_WIDTH = MIX_WIDTH // len(POOL_WINDOWS)
POOL_HIST = max(POOL_WINDOWS) - 1
POOL_PAD = POOL_HIST + 1
DIFF_HEAD_DIM = 64
DIFF_HEADS = MIX_WIDTH // (2 * DIFF_HEAD_DIM)
DIFF_V_DIM = 2 * DIFF_HEAD_DIM
D_FF = 4 * D_MODEL
ROPE_THETA = 10000.0
RMS_EPS = 1e-6
NEG_INF = -1e30

LANES = 128
FF_CHUNK = 1024
VMEM_LIMIT = 56 * 1024 * 1024

BF16 = jnp.bfloat16
F32 = jnp.float32


def _cparams(n_axes):
    return pltpu.CompilerParams(dimension_semantics=("arbitrary",) * n_axes,
                                vmem_limit_bytes=VMEM_LIMIT)


def _const_spec(shape):
    nd = len(shape)
    return pl.BlockSpec(shape, lambda *_: (0,) * nd)


def _rms_unit(x):
    return x * lax.rsqrt(jnp.mean(x * x, axis=-1, keepdims=True) + RMS_EPS)


def _rope_slab(x, cos, sin_lo, sin_hi):
    return (x * cos + pltpu.roll(x, LANES - DIFF_HEAD_DIM // 2, 1) * sin_lo
            + pltpu.roll(x, DIFF_HEAD_DIM // 2, 1) * sin_hi)


def _lambda_value(lq1, lk1, lq2, lk2, j, lam_init):
    a = jnp.sum(lq1[j:j + 1, :] * lk1[j:j + 1, :], axis=-1, keepdims=True)
    b = jnp.sum(lq2[j:j + 1, :] * lk2[j:j + 1, :], axis=-1, keepdims=True)
    return jnp.exp(a) - jnp.exp(b) + lam_init


def _group_select(lane, s2, s4, s8, s16):
    return jnp.where(lane < GROUP_WIDTH, s2 * (1.0 / POOL_WINDOWS[0]),
                     jnp.where(lane < 2 * GROUP_WIDTH, s4 * (1.0 / POOL_WINDOWS[1]),
                               jnp.where(lane < 3 * GROUP_WIDTH, s8 * (1.0 / POOL_WINDOWS[2]),
                                         s16 * (1.0 / POOL_WINDOWS[3]))))


def _norm_mm_kernel(x_ref, g_ref, w_ref, o_ref):
    h = _rms_unit(x_ref[...]) * g_ref[...]
    o_ref[...] = jnp.dot(h.astype(BF16), w_ref[...], preferred_element_type=F32)


def _norm_mm(x, g, w, tm):
    m, d = x.shape
    n = w.shape[1]
    return pl.pallas_call(
        _norm_mm_kernel,
        grid=(m // tm,),
        in_specs=[pl.BlockSpec((tm, d), lambda i: (i, 0)),
                  _const_spec((1, d)),
                  _const_spec((d, n))],
        out_specs=pl.BlockSpec((tm, n), lambda i: (i, 0)),
        out_shape=jax.ShapeDtypeStruct((m, n), F32),
        compiler_params=_cparams(1),
        name="norm_mm",
    )(x, g.reshape(1, d), w)


def _mem_proj_kernel(x_ref, g_ref, w_ref, kt_ref, vt_ref, ktb_ref, vb_ref):
    h = (_rms_unit(x_ref[...]) * g_ref[...]).astype(BF16)
    y = jnp.dot(h, w_ref[...], preferred_element_type=F32)
    for l in range(DEPTH):
        yk = y[:, (2 * l) * MEM_WIDTH:(2 * l + 1) * MEM_WIDTH]
        yv = y[:, (2 * l + 1) * MEM_WIDTH:(2 * l + 2) * MEM_WIDTH]
        ykt = yk.T
        kt_ref[l, 0] = ykt
        ktb_ref[l, 0] = ykt.astype(BF16)
        vt_ref[l, 0] = yv.T
        vb_ref[l, 0] = yv.astype(BF16)


def _mem_proj(x, g, w, batch):
    d = x.shape[1]
    spec = pl.BlockSpec((DEPTH, 1, MEM_WIDTH, MEM_TOKENS), lambda b: (0, b, 0, 0))
    spec_n = pl.BlockSpec((DEPTH, 1, MEM_TOKENS, MEM_WIDTH), lambda b: (0, b, 0, 0))
    return pl.pallas_call(
        _mem_proj_kernel,
        grid=(batch,),
        in_specs=[pl.BlockSpec((MEM_TOKENS, d), lambda b: (b, 0)),
                  _const_spec((1, d)),
                  _const_spec((d, w.shape[1]))],
        out_specs=[spec, spec, spec, spec_n],
        out_shape=[jax.ShapeDtypeStruct((DEPTH, batch, MEM_WIDTH, MEM_TOKENS), F32),
                   jax.ShapeDtypeStruct((DEPTH, batch, MEM_WIDTH, MEM_TOKENS), F32),
                   jax.ShapeDtypeStruct((DEPTH, batch, MEM_WIDTH, MEM_TOKENS), BF16),
                   jax.ShapeDtypeStruct((DEPTH, batch, MEM_TOKENS, MEM_WIDTH), BF16)],
        compiler_params=_cparams(1),
        name="mem_proj",
    )(x, g.reshape(1, d), w)


def _kv_kernel(x_ref, gkv_ref, gn_ref, wk_ref, wv_ref, win_ref, cos_ref, slo_ref, shi_ref,
               kt_ref, vh_ref, z_ref, a_ref, b_ref, *, attention_copies):
    y = _rms_unit(x_ref[...])
    hk = (y * gkv_ref[...]).astype(BF16)
    hn = (y * gn_ref[...]).astype(BF16)
    z_ref[...] = jnp.dot(hn, win_ref[...], preferred_element_type=F32)
    v = jnp.dot(hk, wv_ref[...], preferred_element_type=F32)
    kraw = jnp.dot(hk, wk_ref[...], preferred_element_type=F32)
    cos, slo, shi = cos_ref[...], slo_ref[...], shi_ref[...]
    if not attention_copies:
        b_ref[...] = v
    for h in range(DIFF_HEADS):
        sl = slice(h * LANES, (h + 1) * LANES)
        kr = _rope_slab(kraw[:, sl], cos, slo, shi)
        krt = kr.T
        kt_ref[0, sl, :] = krt
        vh_ref[0, h] = v[:, sl]
        if attention_copies:
            a_ref[0, sl, :] = krt.astype(BF16)
            b_ref[0, h] = v[:, sl].astype(BF16)
        else:
            a_ref[:, sl] = kr


def _kv_proj(x, gkv, gn, wk, wv, win, tabs, n_seq, seq, tm, attention_copies):
    m, d = x.shape
    tps = seq // tm
    n_tab = tabs[0].shape[0] // tm
    tab_spec = pl.BlockSpec((tm, LANES), lambda i: (i % n_tab, 0))
    row = lambda n: pl.BlockSpec((tm, n), lambda i: (i, 0))
    kt_spec = pl.BlockSpec((1, MIX_WIDTH, tm), lambda i: (i // tps, 0, i % tps))
    vh_spec = pl.BlockSpec((1, DIFF_HEADS, tm, LANES), lambda i: (i // tps, 0, i % tps, 0))
    out_specs = [kt_spec, vh_spec, row(D_MODEL)]
    out_shape = [jax.ShapeDtypeStruct((n_seq, MIX_WIDTH, seq), F32),
                 jax.ShapeDtypeStruct((n_seq, DIFF_HEADS, seq, LANES), F32),
                 jax.ShapeDtypeStruct((m, D_MODEL), F32)]
    if attention_copies:
        out_specs += [pl.BlockSpec((1, MIX_WIDTH, tm), lambda i: (i, 0, 0)), vh_spec]
        out_shape += [jax.ShapeDtypeStruct((m // tm, MIX_WIDTH, tm), BF16),
                      jax.ShapeDtypeStruct((n_seq, DIFF_HEADS, seq, LANES), BF16)]
    else:
        out_specs += [row(MIX_WIDTH), row(MIX_WIDTH)]
        out_shape += [jax.ShapeDtypeStruct((m, MIX_WIDTH), F32)] * 2
    return pl.pallas_call(
        functools.partial(_kv_kernel, attention_copies=attention_copies),
        grid=(m // tm,),
        in_specs=[row(d), _const_spec((1, d)), _const_spec((1, d)),
                  _const_spec((d, MIX_WIDTH)), _const_spec((d, MIX_WIDTH)), _const_spec((d, D_MODEL)),
                  tab_spec, tab_spec, tab_spec],
        out_specs=out_specs,
        out_shape=out_shape,
        compiler_params=_cparams(1),
        name="kv_proj",
    )(x, gkv.reshape(1, d), gn.reshape(1, d), wk, wv, win, *tabs)


def _pool_prompt_kernel(u_ref, h_ref, w_ref, scale_ref, o_ref, *, tiles_per_seq, tm):
    t0 = (pl.program_id(0) % tiles_per_seq) * tm
    hist = jnp.where(t0 > 0, h_ref[...], 0.0)
    ext = jnp.concatenate([hist, u_ref[...]], axis=0)
    s2 = ext + pltpu.roll(ext, 1, 0)
    s4 = s2 + pltpu.roll(s2, 2, 0)
    s8 = s4 + pltpu.roll(s4, 4, 0)
    s16 = s8 + pltpu.roll(s8, 8, 0)
    pos = t0 - POOL_PAD + lax.broadcasted_iota(jnp.int32, (tm + POOL_PAD, 1), 0)
    inv = [1.0 / jnp.clip(pos + 1, 1, w).astype(F32) for w in POOL_WINDOWS]
    lane = lax.broadcasted_iota(jnp.int32, (1, MIX_WIDTH), 1)
    mean = jnp.where(lane < GROUP_WIDTH, s2 * inv[0],
                     jnp.where(lane < 2 * GROUP_WIDTH, s4 * inv[1],
                               jnp.where(lane < 3 * GROUP_WIDTH, s8 * inv[2], s16 * inv[3])))
    pooled = (mean - ext)[POOL_PAD:]
    mix = jnp.dot(pooled.astype(BF16), w_ref[...], preferred_element_type=F32) * scale_ref[...]
    o_ref[...] = mix.astype(o_ref.dtype)


def _pool_prompt(z, wbd, scale, seq, tm):
    m = z.shape[0]
    hist_blocks = tm // POOL_PAD
    return pl.pallas_call(
        functools.partial(_pool_prompt_kernel, tiles_per_seq=seq // tm, tm=tm),
        grid=(m // tm,),
        in_specs=[pl.BlockSpec((tm, MIX_WIDTH), lambda i: (i, 0)),
                  pl.BlockSpec((POOL_PAD, MIX_WIDTH), lambda i: (jnp.maximum(i * hist_blocks - 1, 0), 0)),
                  _const_spec((MIX_WIDTH, MIX_WIDTH)),
                  _const_spec((1, MIX_WIDTH))],
        out_specs=pl.BlockSpec((tm, MIX_WIDTH), lambda i: (i, 0)),
        out_shape=jax.ShapeDtypeStruct((m, MIX_WIDTH), BF16),
        compiler_params=_cparams(1),
        name="pool_prompt",
    )(z, z, wbd, scale.reshape(1, MIX_WIDTH))


def _pool_sample_kernel(hist_ref, u_ref, w_ref, scale_ref, mix_ref, hist_out_ref, *, n_tok):
    rows = [hist_ref[0, t] for t in range(POOL_HIST)] + [u_ref[t] for t in range(n_tok)]
    lane = lax.broadcasted_iota(jnp.int32, (1, MIX_WIDTH), 1)
    pooled = []
    for t in range(n_tok):
        c = POOL_HIST + t
        back = lambda a, b: functools.reduce(lambda x, y: x + y, [rows[c - i] for i in range(a, b)])
        s2 = back(0, 2)
        s4 = s2 + back(2, 4)
        s8 = s4 + back(4, 8)
        s16 = s8 + back(8, 16)
        pooled.append((_group_select(lane, s2, s4, s8, s16) - rows[c]).astype(BF16))
    nb = pooled[0].shape[0]
    mix = jnp.dot(jnp.concatenate(pooled, axis=0), w_ref[...], preferred_element_type=F32) * scale_ref[...]
    for t in range(n_tok):
        mix_ref[t] = mix[t * nb:(t + 1) * nb].astype(mix_ref.dtype)
    for t in range(POOL_HIST):
        hist_out_ref[t] = rows[t + n_tok]


def _pool_sample(hist, z3, layer, wbd, scale, bb):
    n_tok, nb, _ = z3.shape
    return pl.pallas_call(
        functools.partial(_pool_sample_kernel, n_tok=n_tok),
        grid=(nb // bb,),
        in_specs=[pl.BlockSpec((1, POOL_HIST, bb, MIX_WIDTH), lambda i: (layer, 0, i, 0)),
                  pl.BlockSpec((n_tok, bb, MIX_WIDTH), lambda i: (0, i, 0)),
                  _const_spec((MIX_WIDTH, MIX_WIDTH)),
                  _const_spec((1, MIX_WIDTH))],
        out_specs=[pl.BlockSpec((n_tok, bb, MIX_WIDTH), lambda i: (0, i, 0)),
                   pl.BlockSpec((POOL_HIST, bb, MIX_WIDTH), lambda i: (0, i, 0))],
        out_shape=[jax.ShapeDtypeStruct((n_tok, nb, MIX_WIDTH), BF16),
                   jax.ShapeDtypeStruct((POOL_HIST, nb, MIX_WIDTH), F32)],
        compiler_params=_cparams(1),
        name="pool_sample",
    )(hist, z3, wbd, scale.reshape(1, MIX_WIDTH))


def _softmax_rows(s):
    e = jnp.exp(s - jnp.max(s, axis=-1, keepdims=True))
    return e * (1.0 / jnp.sum(e, axis=-1, keepdims=True))


def _mem_prompt_kernel(q_ref, mkt_ref, mv_ref, o_ref):
    q = q_ref[...] * (MEM_HEAD_DIM ** -0.5)
    mkt = mkt_ref[0, 0]
    mv = mv_ref[0, 0]
    lane = lax.broadcasted_iota(jnp.int32, (1, MEM_WIDTH), 1)
    ps, mvs = [], []
    for h in range(MEM_HEADS):
        head = (lane >= h * MEM_HEAD_DIM) & (lane < (h + 1) * MEM_HEAD_DIM)
        qh = jnp.where(head, q, 0.0).astype(BF16)
        s = jnp.dot(qh, mkt, preferred_element_type=F32)
        ps.append(_softmax_rows(s).astype(BF16))
        mvs.append(jnp.where(head, mv, jnp.zeros_like(mv)))
    o = jnp.dot(jnp.concatenate(ps, axis=1), jnp.concatenate(mvs, axis=0), preferred_element_type=F32)
    o_ref[...] = o.astype(o_ref.dtype)


def _mem_prompt(z, mkt, mv, layer, seq, tq):
    m = z.shape[0]
    tiles_per_seq = seq // tq
    q_col = MIX_WIDTH // MEM_WIDTH
    kv_spec = pl.BlockSpec((1, 1, MEM_WIDTH, MEM_TOKENS), lambda i: (layer, i // tiles_per_seq, 0, 0))
    return pl.pallas_call(
        _mem_prompt_kernel,
        grid=(m // tq,),
        in_specs=[pl.BlockSpec((tq, MEM_WIDTH), lambda i: (i, q_col)), kv_spec, kv_spec],
        out_specs=pl.BlockSpec((tq, MEM_WIDTH), lambda i: (i, 0)),
        out_shape=jax.ShapeDtypeStruct((m, MEM_WIDTH), BF16),
        compiler_params=_cparams(1),
        name="mem_prompt",
    )(z, mkt, mv)


def _mem_sample_kernel(q_ref, mkt_ref, mvt_ref, o_ref):
    q = q_ref[...]
    mkt = mkt_ref[0].astype(BF16)
    mvt = mvt_ref[0].astype(BF16)
    s = jnp.einsum('bqd,bdm->bqm', q, mkt, preferred_element_type=F32)
    p = _softmax_rows(s).astype(BF16)
    o_ref[...] = jnp.einsum('bqm,bdm->bqd', p, mvt, preferred_element_type=F32)


def _mem_sample(q_rows, cache_kt, cache_vt, layer, bb):
    nb, rows, _ = q_rows.shape
    kv_spec = pl.BlockSpec((1, bb, MEM_WIDTH, MEM_TOKENS), lambda i: (layer, i, 0, 0))
    return pl.pallas_call(
        _mem_sample_kernel,
        grid=(nb // bb,),
        in_specs=[pl.BlockSpec((bb, rows, MEM_WIDTH), lambda i: (i, 0, 0)), kv_spec, kv_spec],
        out_specs=pl.BlockSpec((bb, rows, MEM_WIDTH), lambda i: (i, 0, 0)),
        out_shape=jax.ShapeDtypeStruct((nb, rows, MEM_WIDTH), F32),
        compiler_params=_cparams(1),
        name="mem_sample",
    )(q_rows, cache_kt, cache_vt)


def _subln(o, g_ref, lam_init):
    return _rms_unit(o) * g_ref[...] * (1.0 - lam_init)


def _lane_tiles(x):
    return [x[:, c * LANES:(c + 1) * LANES] for c in range(x.shape[1] // LANES)]


def _diff_prompt_kernel(lq1, lk1, lq2, lk2, g_ref, q_ref, cos_ref, slo_ref, shi_ref, kt_ref, v_ref,
                        o_ref, qm0, qm1, s0, s1, m0, m1, l_ref, acc_ref, *, j, lam_init, tq, nq):
    i = pl.program_id(2)

    def prologue(qm, m):
        q = _rope_slab(q_ref[...], cos_ref[...], slo_ref[...], shi_ref[...]) * (DIFF_HEAD_DIM ** -0.5)
        lane = lax.broadcasted_iota(jnp.int32, (1, LANES), 1)
        qm[0] = jnp.where(lane < DIFF_HEAD_DIM, q, 0.0).astype(BF16)
        qm[1] = jnp.where(lane >= DIFF_HEAD_DIM, q, 0.0).astype(BF16)
        m[...] = jnp.full(m.shape, NEG_INF, F32)

    def scores(qm, s_buf, m, kv, masked):
        kt = kt_ref[0, kv]
        for a in range(2):
            s = jnp.dot(qm[a], kt, preferred_element_type=F32)
            if masked:
                row = lax.broadcasted_iota(jnp.int32, (tq, tq), 0)
                col = lax.broadcasted_iota(jnp.int32, (tq, tq), 1)
                s = jnp.where(col <= row, s, NEG_INF)
            s_buf[a, kv] = s
            m[a] = functools.reduce(jnp.maximum, _lane_tiles(s), m[a])

    def finish_max(m):
        for a in range(2):
            m[a] = jnp.broadcast_to(jnp.max(m[a], axis=-1, keepdims=True), (tq, LANES))

    def clear_sums():
        l_ref[...] = jnp.zeros(l_ref.shape, F32)
        acc_ref[...] = jnp.zeros(acc_ref.shape, F32)

    def weigh(s_buf, m, kv):
        v = v_ref[0, 0, pl.ds(pl.multiple_of(kv * tq, tq), tq), :]
        for a in range(2):
            row_max = m[a]
            p = [jnp.exp(t - row_max) for t in _lane_tiles(s_buf[a, kv])]
            l_ref[a] += functools.reduce(lambda x, y: x + y, p)
            acc_ref[a] += jnp.dot(jnp.concatenate(p, axis=1).astype(BF16), v, preferred_element_type=F32)

    def epilogue():
        lam = _lambda_value(lq1[...], lk1[...], lq2[...], lk2[...], j, lam_init)
        inv = [1.0 / jnp.sum(l_ref[a], axis=-1, keepdims=True) for a in range(2)]
        o = acc_ref[0] * inv[0] - lam * (acc_ref[1] * inv[1])
        o_ref[...] = _subln(o, g_ref, lam_init).astype(o_ref.dtype)

    def middle(qm, s_cur, m_cur, s_prev, m_prev, odd):
        prologue(qm, m_cur)
        clear_sums()

        def both(kv):
            scores(qm, s_cur, m_cur, kv, False)
            weigh(s_prev, m_prev, kv)

        def body(pair, carry):
            both(2 * pair)
            both(2 * pair + 1)
            return carry

        lax.fori_loop(0, lax.shift_right_logical(i, 1), body, 0)
        if odd:
            both(i - 1)
        scores(qm, s_cur, m_cur, i, True)
        finish_max(m_cur)
        epilogue()

    @pl.when(i == 0)
    def _():
        prologue(qm0, m0)
        scores(qm0, s0, m0, 0, True)
        finish_max(m0)

    @pl.when((i > 0) & (i < nq) & (i % 2 == 0))
    def _():
        middle(qm0, s0, m0, s1, m1, False)

    @pl.when((i < nq) & (i % 2 == 1))
    def _():
        middle(qm1, s1, m1, s0, m0, True)

    @pl.when(i == nq)
    def _():
        s_prev, m_prev = (s1, m1) if nq % 2 == 0 else (s0, m0)
        clear_sums()

        def body(pair, carry):
            weigh(s_prev, m_prev, 2 * pair)
            weigh(s_prev, m_prev, 2 * pair + 1)
            return carry

        lax.fori_loop(0, nq // 2, body, 0)
        if nq % 2:
            weigh(s_prev, m_prev, nq - 1)
        epilogue()


def _diff_prompt(z, kt, vb, tabs, lams, g, j, lam_init, batch, seq, tq):
    m = z.shape[0]
    nq = seq // tq
    lam_spec = _const_spec((N_B, DIFF_HEAD_DIM))
    cur = lambda i: jnp.minimum(i, nq - 1)
    done = lambda i: jnp.maximum(i - 1, 0)
    tab_spec = pl.BlockSpec((tq, LANES), lambda b, h, i: (cur(i), 0))
    return pl.pallas_call(
        functools.partial(_diff_prompt_kernel, j=j, lam_init=lam_init, tq=tq, nq=nq),
        grid=(batch, DIFF_HEADS, nq + 1),
        in_specs=[lam_spec, lam_spec, lam_spec, lam_spec,
                  _const_spec((1, DIFF_V_DIM)),
                  pl.BlockSpec((tq, LANES), lambda b, h, i: (b * nq + cur(i), h)),
                  tab_spec, tab_spec, tab_spec,
                  pl.BlockSpec((1, nq, LANES, tq), lambda b, h, i: (b, 0, h, 0)),
                  pl.BlockSpec((1, 1, seq, LANES), lambda b, h, i: (b, h, 0, 0))],
        out_specs=pl.BlockSpec((tq, LANES), lambda b, h, i: (b * nq + done(i), h)),
        out_shape=jax.ShapeDtypeStruct((m, MIX_WIDTH), BF16),
        scratch_shapes=[pltpu.VMEM((2, tq, LANES), BF16), pltpu.VMEM((2, tq, LANES), BF16),
                        pltpu.VMEM((2, nq, tq, tq), F32), pltpu.VMEM((2, nq, tq, tq), F32),
                        pltpu.VMEM((2, tq, LANES), F32), pltpu.VMEM((2, tq, LANES), F32),
                        pltpu.VMEM((2, tq, LANES), F32),
                        pltpu.VMEM((2, tq, LANES), F32)],
        compiler_params=_cparams(3),
        name="diff_prompt",
    )(*lams, g.reshape(1, DIFF_V_DIM), z, *tabs, kt.reshape(batch, nq, MIX_WIDTH, tq), vb)


def _diff_sample_kernel(pt_ref, lq1, lk1, lq2, lk2, g_ref, z_ref, cos_ref, slo_ref, shi_ref,
                        kn_ref, vn_ref, *rest, j, lam_init, n_pages, n_tok):
    del pt_ref
    k_pages = rest[:n_pages]
    v_pages = rest[n_pages:2 * n_pages]
    o_ref, x_scr, pg_scr, kb, vb = rest[2 * n_pages:]
    past = n_pages * PAGE_SIZE
    rows = 2 * n_tok

    x_scr[...] = jnp.zeros(x_scr.shape, F32)
    x_scr[0:n_tok, :] = z_ref[0]
    x = x_scr[...]
    x = x + pltpu.roll(x, n_tok, 0)

    for r in range(n_pages):
        cols = slice(r * PAGE_SIZE, (r + 1) * PAGE_SIZE)
        for h in range(DIFF_HEADS):
            kb[h, :, cols] = k_pages[r][0, h * LANES:(h + 1) * LANES, :].astype(BF16)
            vb[h, cols, :] = v_pages[r][0, h].astype(BF16)
    new = slice(past, past + PAGE_SIZE)
    pg_scr[...] = jnp.zeros(pg_scr.shape, F32)
    pg_scr[0:n_tok, :] = kn_ref[0]
    for h in range(DIFF_HEADS):
        kb[h, :, new] = pg_scr[:, h * LANES:(h + 1) * LANES].T.astype(BF16)
    pg_scr[0:n_tok, :] = vn_ref[0]
    for h in range(DIFF_HEADS):
        vb[h, new, :] = pg_scr[:, h * LANES:(h + 1) * LANES].astype(BF16)

    lam = _lambda_value(lq1[...], lk1[...], lq2[...], lk2[...], j, lam_init)
    row = lax.broadcasted_iota(jnp.int32, (rows, LANES), 0)
    lane = lax.broadcasted_iota(jnp.int32, (rows, LANES), 1)
    own_map = (row < n_tok) == (lane < DIFF_HEAD_DIM)
    n_keys = past + PAGE_SIZE
    key = lax.broadcasted_iota(jnp.int32, (rows, n_keys), 1)
    tok = lax.broadcasted_iota(jnp.int32, (rows, n_keys), 0) % n_tok
    visible = key - past <= tok
    cos, slo, shi = cos_ref[...], slo_ref[...], shi_ref[...]

    for h in range(DIFF_HEADS):
        sl = slice(h * LANES, (h + 1) * LANES)
        q = _rope_slab(x[:, sl], cos, slo, shi) * (DIFF_HEAD_DIM ** -0.5)
        q = jnp.where(own_map, q, 0.0).astype(BF16)
        s = jnp.dot(q, kb[h], preferred_element_type=F32)
        p = _softmax_rows(jnp.where(visible, s, NEG_INF))
        a = p - lam * pltpu.roll(p, n_tok, 0)
        o = jnp.dot(a.astype(BF16), vb[h], preferred_element_type=F32)
        o_ref[0, :, sl] = _subln(o[0:n_tok], g_ref, lam_init).astype(o_ref.dtype)


def _diff_sample(page_table, zb, tabs, k_new, v_new, cache_kt, cache_vh, lams, g, j, lam_init):
    nb, n_tok, _ = zb.shape
    n_pages = page_table.shape[1]
    rows = 2 * n_tok
    n_keys = (n_pages + 1) * PAGE_SIZE
    lam_spec = _const_spec((N_B, DIFF_HEAD_DIM))
    tab_spec = _const_spec((rows, LANES))
    tok_spec = pl.BlockSpec((1, n_tok, MIX_WIDTH), lambda b, pt: (b, 0, 0))

    def k_spec(r):
        return pl.BlockSpec((1, MIX_WIDTH, PAGE_SIZE), lambda b, pt: (pt[b, r], 0, 0))

    def v_spec(r):
        return pl.BlockSpec((1, DIFF_HEADS, PAGE_SIZE, LANES), lambda b, pt: (pt[b, r], 0, 0, 0))

    grid_spec = pltpu.PrefetchScalarGridSpec(
        num_scalar_prefetch=1,
        grid=(nb,),
        in_specs=[lam_spec, lam_spec, lam_spec, lam_spec, _const_spec((1, DIFF_V_DIM)),
                  tok_spec, tab_spec, tab_spec, tab_spec, tok_spec, tok_spec]
                 + [k_spec(r) for r in range(n_pages)] + [v_spec(r) for r in range(n_pages)],
        out_specs=tok_spec,
        scratch_shapes=[pltpu.VMEM((rows, MIX_WIDTH), F32),
                        pltpu.VMEM((PAGE_SIZE, MIX_WIDTH), F32),
                        pltpu.VMEM((DIFF_HEADS, LANES, n_keys), BF16),
                        pltpu.VMEM((DIFF_HEADS, n_keys, LANES), BF16)],
    )
    return pl.pallas_call(
        functools.partial(_diff_sample_kernel, j=j, lam_init=lam_init, n_pages=n_pages, n_tok=n_tok),
        grid_spec=grid_spec,
        out_shape=jax.ShapeDtypeStruct((nb, n_tok, MIX_WIDTH), BF16),
        compiler_params=_cparams(1),
        name="diff_sample",
    )(page_table, *lams, g.reshape(1, DIFF_V_DIM), zb, *tabs, k_new, v_new,
      *([cache_kt] * n_pages), *([cache_vh] * n_pages))


def _post_kernel(x_ref, mix_ref, mem_ref, woa_ref, wob_ref, g2_ref, w1_ref, w2_ref, *rest, final):
    if final:
        gf_ref, o_ref = rest
    else:
        (o_ref,) = rest
    o_ref[...] = (x_ref[...]
                  + jnp.dot(mix_ref[...], woa_ref[...], preferred_element_type=F32)
                  + jnp.dot(mem_ref[...], wob_ref[...], preferred_element_type=F32))
    h = (_rms_unit(o_ref[...]) * g2_ref[...]).astype(BF16)
    for c in range(D_FF // FF_CHUNK):
        sl = slice(c * FF_CHUNK, (c + 1) * FF_CHUNK)
        a = jnp.maximum(jnp.dot(h, w1_ref[:, sl], preferred_element_type=F32), 0.0)
        o_ref[...] += jnp.dot((a * a).astype(BF16), w2_ref[sl, :], preferred_element_type=F32)
    if final:
        o_ref[...] = _rms_unit(o_ref[...]) * gf_ref[...]


def _post(x, mix, mem, woa, wob, g2, w1, w2, gf, tm):
    m, d = x.shape
    final = gf is not None
    row = lambda n: pl.BlockSpec((tm, n), lambda i: (i, 0))
    in_specs = [row(d), row(MIX_WIDTH), row(MEM_WIDTH),
                _const_spec((MIX_WIDTH, d)), _const_spec((MEM_WIDTH, d)), _const_spec((1, d)),
                _const_spec((d, D_FF)), _const_spec((D_FF, d))]
    args = [x, mix, mem, woa, wob, g2.reshape(1, d), w1, w2]
    if final:
        in_specs.append(_const_spec((1, d)))
        args.append(gf.reshape(1, d))
    return pl.pallas_call(
        functools.partial(_post_kernel, final=final),
        grid=(m // tm,),
        in_specs=in_specs,
        out_specs=row(d),
        out_shape=jax.ShapeDtypeStruct((m, d), F32),
        compiler_params=_cparams(1),
        name="post_final" if final else "post",
    )(*args)


def _rope_tables(pos):
    inv = 1.0 / (ROPE_THETA ** (jnp.arange(0, DIFF_HEAD_DIM, 2, dtype=F32) / DIFF_HEAD_DIM))
    ang = pos.astype(F32)[:, None] * inv[None, :]
    cos, sin = jnp.cos(ang), jnp.sin(ang)
    zero = jnp.zeros_like(sin)
    tile2 = lambda a, b: jnp.concatenate([a, b, a, b], axis=-1)
    return tile2(cos, cos), tile2(-sin, zero), tile2(zero, sin)


def _lam_init(layer):
    return 0.8 - 0.6 * math.exp(-0.3 * layer)


def kernel(x_prompt, x_sample, state_pool, cache_mem_k, cache_mem_v, cache_k, cache_v, page_table, mem_prompt,
           norm1, norm2, final_norm, kv_norm, mem_norm, w_in, w_out, w_mem_k, w_mem_v, w_pool, pool_scale,
           w_k, w_v, lambda_q1, lambda_k1, lambda_q2, lambda_k2, subln, w_mlp1, w_mlp2):
    bp, tp, d = x_prompt.shape
    bs, ts, _ = x_sample.shape
    n_pages = page_table.shape[1]
    assert d == D_MODEL and n_pages * PAGE_SIZE == PAST_LEN and PAST_LEN >= POOL_HIST

    w_in_b = w_in.astype(BF16)
    woa_b = w_out[:, :MIX_WIDTH].astype(BF16)
    wob_b = w_out[:, MIX_WIDTH:].astype(BF16)
    w1_b = w_mlp1.astype(BF16)
    w2_b = w_mlp2.astype(BF16)
    wk_b = w_k.astype(BF16)
    wv_b = w_v.astype(BF16)
    w_memkv = jnp.concatenate([w_mem_k, w_mem_v], axis=-1)
    w_memkv = w_memkv.transpose(1, 0, 2).reshape(d, DEPTH * 2 * MEM_WIDTH).astype(BF16)
    wbd = jnp.zeros((N_A, MIX_WIDTH, MIX_WIDTH), F32)
    for g in range(len(POOL_WINDOWS)):
        sl = slice(g * GROUP_WIDTH, (g + 1) * GROUP_WIDTH)
        wbd = wbd.at[:, sl, sl].set(w_pool[:, g])
    wbd = wbd.astype(BF16)
    lams = (lambda_q1, lambda_k1, lambda_q2, lambda_k2)

    tm = 512
    mp = bp * tp
    mem_kt, mem_vt, mem_kt_b, mem_v_b = _mem_proj(mem_prompt.reshape(bp * MEM_TOKENS, d), mem_norm, w_memkv, bp)
    to_heads = lambda a: a.reshape(DEPTH, bp, MEM_HEADS, MEM_HEAD_DIM, MEM_TOKENS).transpose(0, 1, 4, 2, 3)
    mem_k_prompt, mem_v_prompt = to_heads(mem_kt), to_heads(mem_vt)

    tabs_p = _rope_tables(jnp.arange(tp, dtype=jnp.int32))
    x = x_prompt.reshape(mp, d)
    pool_prompt = []
    z = _norm_mm(x, norm1[0], w_in_b[0], tm)
    for l in range(DEPTH):
        if l < N_A:
            pool_prompt.append(z.reshape(bp, tp, d)[:, tp - POOL_HIST:, :MIX_WIDTH])
            mix = _pool_prompt(z, wbd[l], pool_scale[l], tp, tm)
        else:
            j = l - N_A
            mix = _diff_prompt(z, kt_b, vh_b, tabs_p, lams, subln[j], j, _lam_init(l), bp, tp, tm)
        mem = _mem_prompt(z, mem_kt_b, mem_v_b, l, tp, tm)
        gf = final_norm if l == DEPTH - 1 else None
        x = _post(x, mix, mem, woa_b[l], wob_b[l], norm2[l], w1_b[l], w2_b[l], gf, tm)
        if l + 1 == N_A:
            kt_p, vh_p, z, kt_b, vh_b = _kv_proj(x, kv_norm, norm1[l + 1], wk_b, wv_b, w_in_b[l + 1],
                                                 tabs_p, bp, tp, tm, True)
        elif l + 1 < DEPTH:
            z = _norm_mm(x, norm1[l + 1], w_in_b[l + 1], tm)
    y_prompt = x.reshape(bp, tp, d)
    pool_prompt = jnp.stack(pool_prompt, axis=0)
    k_prompt = kt_p.reshape(bp, DIFF_HEADS, 2, DIFF_HEAD_DIM, tp).transpose(0, 4, 1, 2, 3)
    v_prompt = vh_p.transpose(0, 2, 1, 3)

    ms = bs * ts
    pos_s = PAST_LEN + jnp.arange(ts, dtype=jnp.int32)
    tabs_s = _rope_tables(jnp.repeat(pos_s, bs))
    tabs_s2 = _rope_tables(jnp.tile(pos_s, 2))
    ck_t = cache_k.transpose(0, 2, 3, 4, 1).reshape(cache_k.shape[0], MIX_WIDTH, PAGE_SIZE)
    cv_h = cache_v.transpose(0, 2, 1, 3)
    cmk_t = cache_mem_k.transpose(0, 1, 3, 4, 2).reshape(DEPTH, bs, MEM_WIDTH, MEM_TOKENS)
    cmv_t = cache_mem_v.transpose(0, 1, 3, 4, 2).reshape(DEPTH, bs, MEM_WIDTH, MEM_TOKENS)
    hist = state_pool.transpose(0, 2, 1, 3)
    head_of_col = jnp.arange(MEM_WIDTH) // MEM_HEAD_DIM
    head_mask = (head_of_col[None, :] == jnp.arange(MEM_HEADS)[:, None]).astype(F32)
    to_requests = lambda a: a.reshape(ts, bs, -1).transpose(1, 0, 2)

    x = x_sample.transpose(1, 0, 2).reshape(ms, d)
    pool_sample = []
    z = _norm_mm(x, norm1[0], w_in_b[0], ms)
    for l in range(DEPTH):
        z3 = z.reshape(ts, bs, d)
        if l < N_A:
            mix, kept = _pool_sample(hist, z3, l, wbd[l], pool_scale[l], 32)
            pool_sample.append(kept)
            mix = mix.reshape(ms, MIX_WIDTH)
        else:
            j = l - N_A
            mix = _diff_sample(page_table, to_requests(z[:, :MIX_WIDTH]), tabs_s2, to_requests(k_s),
                               to_requests(v_s), ck_t, cv_h, lams, subln[j], j, _lam_init(l))
            mix = mix.transpose(1, 0, 2).reshape(ms, MIX_WIDTH)
        qm = to_requests(z[:, MIX_WIDTH:])[:, None] * (MEM_HEAD_DIM ** -0.5) * head_mask[None, :, None, :]
        om = _mem_sample(qm.reshape(bs, MEM_HEADS * ts, MEM_WIDTH).astype(BF16), cmk_t, cmv_t, l, 16)
        mem = jnp.sum(om.reshape(bs, MEM_HEADS, ts, MEM_WIDTH) * head_mask[None, :, None, :], axis=1)
        mem = mem.transpose(1, 0, 2).reshape(ms, MEM_WIDTH).astype(BF16)
        gf = final_norm if l == DEPTH - 1 else None
        x = _post(x, mix, mem, woa_b[l], wob_b[l], norm2[l], w1_b[l], w2_b[l], gf, ms // 2)
        if l + 1 == N_A:
            kt_s, vh_s, z, k_s, v_s = _kv_proj(x, kv_norm, norm1[l + 1], wk_b, wv_b, w_in_b[l + 1],
                                               tabs_s, ts, bs, bs, False)
        elif l + 1 < DEPTH:
            z = _norm_mm(x, norm1[l + 1], w_in_b[l + 1], ms)
    y_sample = x.reshape(ts, bs, d).transpose(1, 0, 2)
    pool_sample = jnp.stack(pool_sample, axis=0).transpose(0, 2, 1, 3)
    k_sample = kt_s.reshape(ts, DIFF_HEADS, 2, DIFF_HEAD_DIM, bs).transpose(4, 0, 1, 2, 3)
    v_sample = vh_s.transpose(2, 0, 1, 3)

    return (y_prompt, y_sample, pool_prompt, mem_k_prompt, mem_v_prompt, k_prompt, v_prompt,
            pool_sample, k_sample, v_sample)
```

```python
import functools
import math

import jax
import jax.numpy as jnp
from jax import lax
from jax.experimental import pallas as pl
from jax.experimental.pallas import tpu as pltpu

D_MODEL = 1024
DEPTH = 4
N_A = DEPTH // 2
N_B = DEPTH - N_A
PAST_LEN = 2048
PAGE_SIZE = 128
MEM_TOKENS = 256
MEM_HEADS = 4
MEM_HEAD_DIM = D_MODEL // 16
MEM_WIDTH = MEM_HEADS * MEM_HEAD_DIM
MIX_WIDTH = D_MODEL - MEM_WIDTH
POOL_WINDOWS = (2, 4, 8, 16)
GROUP_WIDTH = MIX_WIDTH // len(POOL_WINDOWS)
POOL_HIST = max(POOL_WINDOWS) - 1
POOL_PAD = POOL_HIST + 1
DIFF_HEAD_DIM = 64
DIFF_HEADS = MIX_WIDTH // (2 * DIFF_HEAD_DIM)
DIFF_V_DIM = 2 * DIFF_HEAD_DIM
D_FF = 4 * D_MODEL
ROPE_THETA = 10000.0
RMS_EPS = 1e-6
NEG_INF = -1e30

LANES = 128
FF_CHUNK = 1024
VMEM_LIMIT = 56 * 1024 * 1024
FUSED_VMEM_LIMIT = 60 * 1024 * 1024

BF16 = jnp.bfloat16
F32 = jnp.float32


def _cparams(n_axes):
    return pltpu.CompilerParams(dimension_semantics=("arbitrary",) * n_axes,
                                vmem_limit_bytes=VMEM_LIMIT)


def _const_spec(shape):
    nd = len(shape)
    return pl.BlockSpec(shape, lambda *_: (0,) * nd)


def _rms_unit(x):
    return x * lax.rsqrt(jnp.mean(x * x, axis=-1, keepdims=True) + RMS_EPS)


def _rope_slab(x, cos, sin_lo, sin_hi):
    return (x * cos + pltpu.roll(x, LANES - DIFF_HEAD_DIM // 2, 1) * sin_lo
            + pltpu.roll(x, DIFF_HEAD_DIM // 2, 1) * sin_hi)


def _lambda_value(lq1, lk1, lq2, lk2, j, lam_init):
    a = jnp.sum(lq1[j:j + 1, :] * lk1[j:j + 1, :], axis=-1, keepdims=True)
    b = jnp.sum(lq2[j:j + 1, :] * lk2[j:j + 1, :], axis=-1, keepdims=True)
    return jnp.exp(a) - jnp.exp(b) + lam_init


def _group_select(lane, s2, s4, s8, s16):
    return jnp.where(lane < GROUP_WIDTH, s2 * (1.0 / POOL_WINDOWS[0]),
                     jnp.where(lane < 2 * GROUP_WIDTH, s4 * (1.0 / POOL_WINDOWS[1]),
                               jnp.where(lane < 3 * GROUP_WIDTH, s8 * (1.0 / POOL_WINDOWS[2]),
                                         s16 * (1.0 / POOL_WINDOWS[3]))))


def _norm_mm_kernel(x_ref, g_ref, w_ref, o_ref):
    h = _rms_unit(x_ref[...]) * g_ref[...]
    o_ref[...] = jnp.dot(h.astype(BF16), w_ref[...], preferred_element_type=F32)


def _norm_mm(x, g, w, tm):
    m, d = x.shape
    n = w.shape[1]
    return pl.pallas_call(
        _norm_mm_kernel,
        grid=(m // tm,),
        in_specs=[pl.BlockSpec((tm, d), lambda i: (i, 0)),
                  _const_spec((1, d)),
                  _const_spec((d, n))],
        out_specs=pl.BlockSpec((tm, n), lambda i: (i, 0)),
        out_shape=jax.ShapeDtypeStruct((m, n), F32),
        compiler_params=_cparams(1),
        name="norm_mm",
    )(x, g.reshape(1, d), w)


def _mem_proj_kernel(x_ref, g_ref, w_ref, kt_ref, vt_ref, ktb_ref, vb_ref):
    h = (_rms_unit(x_ref[...]) * g_ref[...]).astype(BF16)
    y = jnp.dot(h, w_ref[...], preferred_element_type=F32)
    for l in range(DEPTH):
        yk = y[:, (2 * l) * MEM_WIDTH:(2 * l + 1) * MEM_WIDTH]
        yv = y[:, (2 * l + 1) * MEM_WIDTH:(2 * l + 2) * MEM_WIDTH]
        ykt = yk.T
        kt_ref[l, 0] = ykt
        ktb_ref[l, 0] = ykt.astype(BF16)
        vt_ref[l, 0] = yv.T
        vb_ref[l, 0] = yv.astype(BF16)


def _mem_proj(x, g, w, batch):
    d = x.shape[1]
    spec = pl.BlockSpec((DEPTH, 1, MEM_WIDTH, MEM_TOKENS), lambda b: (0, b, 0, 0))
    spec_n = pl.BlockSpec((DEPTH, 1, MEM_TOKENS, MEM_WIDTH), lambda b: (0, b, 0, 0))
    return pl.pallas_call(
        _mem_proj_kernel,
        grid=(batch,),
        in_specs=[pl.BlockSpec((MEM_TOKENS, d), lambda b: (b, 0)),
                  _const_spec((1, d)),
                  _const_spec((d, w.shape[1]))],
        out_specs=[spec, spec, spec, spec_n],
        out_shape=[jax.ShapeDtypeStruct((DEPTH, batch, MEM_WIDTH, MEM_TOKENS), F32),
                   jax.ShapeDtypeStruct((DEPTH, batch, MEM_WIDTH, MEM_TOKENS), F32),
                   jax.ShapeDtypeStruct((DEPTH, batch, MEM_WIDTH, MEM_TOKENS), BF16),
                   jax.ShapeDtypeStruct((DEPTH, batch, MEM_TOKENS, MEM_WIDTH), BF16)],
        compiler_params=_cparams(1),
        name="mem_proj",
    )(x, g.reshape(1, d), w)


def _kv_kernel(x_ref, gkv_ref, gn_ref, wk_ref, wv_ref, win_ref, cos_ref, slo_ref, shi_ref,
               kt_ref, vh_ref, z_ref, a_ref, b_ref, *, attention_copies):
    y = _rms_unit(x_ref[...])
    hk = (y * gkv_ref[...]).astype(BF16)
    hn = (y * gn_ref[...]).astype(BF16)
    z_ref[...] = jnp.dot(hn, win_ref[...], preferred_element_type=F32)
    v = jnp.dot(hk, wv_ref[...], preferred_element_type=F32)
    kraw = jnp.dot(hk, wk_ref[...], preferred_element_type=F32)
    cos, slo, shi = cos_ref[...], slo_ref[...], shi_ref[...]
    if not attention_copies:
        b_ref[...] = v
    for h in range(DIFF_HEADS):
        sl = slice(h * LANES, (h + 1) * LANES)
        kr = _rope_slab(kraw[:, sl], cos, slo, shi)
        krt = kr.T
        kt_ref[0, sl, :] = krt
        vh_ref[0, h] = v[:, sl]
        if attention_copies:
            a_ref[0, sl, :] = krt.astype(BF16)
            b_ref[0, h] = v[:, sl].astype(BF16)
        else:
            a_ref[:, sl] = kr


def _kv_proj(x, gkv, gn, wk, wv, win, tabs, n_seq, seq, tm, attention_copies):
    m, d = x.shape
    tps = seq // tm
    n_tab = tabs[0].shape[0] // tm
    tab_spec = pl.BlockSpec((tm, LANES), lambda i: (i % n_tab, 0))
    row = lambda n: pl.BlockSpec((tm, n), lambda i: (i, 0))
    kt_spec = pl.BlockSpec((1, MIX_WIDTH, tm), lambda i: (i // tps, 0, i % tps))
    vh_spec = pl.BlockSpec((1, DIFF_HEADS, tm, LANES), lambda i: (i // tps, 0, i % tps, 0))
    out_specs = [kt_spec, vh_spec, row(D_MODEL)]
    out_shape = [jax.ShapeDtypeStruct((n_seq, MIX_WIDTH, seq), F32),
                 jax.ShapeDtypeStruct((n_seq, DIFF_HEADS, seq, LANES), F32),
                 jax.ShapeDtypeStruct((m, D_MODEL), F32)]
    if attention_copies:
        out_specs += [pl.BlockSpec((1, MIX_WIDTH, tm), lambda i: (i, 0, 0)), vh_spec]
        out_shape += [jax.ShapeDtypeStruct((m // tm, MIX_WIDTH, tm), BF16),
                      jax.ShapeDtypeStruct((n_seq, DIFF_HEADS, seq, LANES), BF16)]
    else:
        out_specs += [row(MIX_WIDTH), row(MIX_WIDTH)]
        out_shape += [jax.ShapeDtypeStruct((m, MIX_WIDTH), F32)] * 2
    return pl.pallas_call(
        functools.partial(_kv_kernel, attention_copies=attention_copies),
        grid=(m // tm,),
        in_specs=[row(d), _const_spec((1, d)), _const_spec((1, d)),
                  _const_spec((d, MIX_WIDTH)), _const_spec((d, MIX_WIDTH)), _const_spec((d, D_MODEL)),
                  tab_spec, tab_spec, tab_spec],
        out_specs=out_specs,
        out_shape=out_shape,
        compiler_params=_cparams(1),
        name="kv_proj",
    )(x, gkv.reshape(1, d), gn.reshape(1, d), wk, wv, win, *tabs)


def _pool_prompt_kernel(u_ref, h_ref, w_ref, scale_ref, o_ref, *, tiles_per_seq, tm):
    t0 = (pl.program_id(0) % tiles_per_seq) * tm
    hist = jnp.where(t0 > 0, h_ref[...], 0.0)
    ext = jnp.concatenate([hist, u_ref[...]], axis=0)
    s2 = ext + pltpu.roll(ext, 1, 0)
    s4 = s2 + pltpu.roll(s2, 2, 0)
    s8 = s4 + pltpu.roll(s4, 4, 0)
    s16 = s8 + pltpu.roll(s8, 8, 0)
    pos = t0 - POOL_PAD + lax.broadcasted_iota(jnp.int32, (tm + POOL_PAD, 1), 0)
    inv = [1.0 / jnp.clip(pos + 1, 1, w).astype(F32) for w in POOL_WINDOWS]
    lane = lax.broadcasted_iota(jnp.int32, (1, MIX_WIDTH), 1)
    mean = jnp.where(lane < GROUP_WIDTH, s2 * inv[0],
                     jnp.where(lane < 2 * GROUP_WIDTH, s4 * inv[1],
                               jnp.where(lane < 3 * GROUP_WIDTH, s8 * inv[2], s16 * inv[3])))
    pooled = (mean - ext)[POOL_PAD:]
    mix = jnp.dot(pooled.astype(BF16), w_ref[...], preferred_element_type=F32) * scale_ref[...]
    o_ref[...] = mix.astype(o_ref.dtype)


def _pool_prompt(z, wbd, scale, seq, tm):
    m = z.shape[0]
    hist_blocks = tm // POOL_PAD
    return pl.pallas_call(
        functools.partial(_pool_prompt_kernel, tiles_per_seq=seq // tm, tm=tm),
        grid=(m // tm,),
        in_specs=[pl.BlockSpec((tm, MIX_WIDTH), lambda i: (i, 0)),
                  pl.BlockSpec((POOL_PAD, MIX_WIDTH), lambda i: (jnp.maximum(i * hist_blocks - 1, 0), 0)),
                  _const_spec((MIX_WIDTH, MIX_WIDTH)),
                  _const_spec((1, MIX_WIDTH))],
        out_specs=pl.BlockSpec((tm, MIX_WIDTH), lambda i: (i, 0)),
        out_shape=jax.ShapeDtypeStruct((m, MIX_WIDTH), BF16),
        compiler_params=_cparams(1),
        name="pool_prompt",
    )(z, z, wbd, scale.reshape(1, MIX_WIDTH))


def _pool_sample_kernel(hist_ref, u_ref, w_ref, scale_ref, mix_ref, hist_out_ref, *, n_tok):
    rows = [hist_ref[0, t] for t in range(POOL_HIST)] + [u_ref[t] for t in range(n_tok)]
    lane = lax.broadcasted_iota(jnp.int32, (1, MIX_WIDTH), 1)
    pooled = []
    for t in range(n_tok):
        c = POOL_HIST + t
        back = lambda a, b: functools.reduce(lambda x, y: x + y, [rows[c - i] for i in range(a, b)])
        s2 = back(0, 2)
        s4 = s2 + back(2, 4)
        s8 = s4 + back(4, 8)
        s16 = s8 + back(8, 16)
        pooled.append((_group_select(lane, s2, s4, s8, s16) - rows[c]).astype(BF16))
    nb = pooled[0].shape[0]
    mix = jnp.dot(jnp.concatenate(pooled, axis=0), w_ref[...], preferred_element_type=F32) * scale_ref[...]
    for t in range(n_tok):
        mix_ref[t] = mix[t * nb:(t + 1) * nb].astype(mix_ref.dtype)
    for t in range(POOL_HIST):
        hist_out_ref[t] = rows[t + n_tok]


def _pool_sample(hist, z3, layer, wbd, scale, bb):
    n_tok, nb, _ = z3.shape
    return pl.pallas_call(
        functools.partial(_pool_sample_kernel, n_tok=n_tok),
        grid=(nb // bb,),
        in_specs=[pl.BlockSpec((1, POOL_HIST, bb, MIX_WIDTH), lambda i: (layer, 0, i, 0)),
                  pl.BlockSpec((n_tok, bb, MIX_WIDTH), lambda i: (0, i, 0)),
                  _const_spec((MIX_WIDTH, MIX_WIDTH)),
                  _const_spec((1, MIX_WIDTH))],
        out_specs=[pl.BlockSpec((n_tok, bb, MIX_WIDTH), lambda i: (0, i, 0)),
                   pl.BlockSpec((POOL_HIST, bb, MIX_WIDTH), lambda i: (0, i, 0))],
        out_shape=[jax.ShapeDtypeStruct((n_tok, nb, MIX_WIDTH), BF16),
                   jax.ShapeDtypeStruct((POOL_HIST, nb, MIX_WIDTH), F32)],
        compiler_params=_cparams(1),
        name="pool_sample",
    )(hist, z3, wbd, scale.reshape(1, MIX_WIDTH))


def _softmax_rows(s):
    e = jnp.exp(s - jnp.max(s, axis=-1, keepdims=True))
    return e * (1.0 / jnp.sum(e, axis=-1, keepdims=True))


def _mem_prompt_kernel(q_ref, mkt_ref, mv_ref, o_ref):
    q = q_ref[...] * (MEM_HEAD_DIM ** -0.5)
    mkt = mkt_ref[0, 0]
    mv = mv_ref[0, 0]
    lane = lax.broadcasted_iota(jnp.int32, (1, MEM_WIDTH), 1)
    ps, mvs = [], []
    for h in range(MEM_HEADS):
        head = (lane >= h * MEM_HEAD_DIM) & (lane < (h + 1) * MEM_HEAD_DIM)
        qh = jnp.where(head, q, 0.0).astype(BF16)
        s = jnp.dot(qh, mkt, preferred_element_type=F32)
        ps.append(_softmax_rows(s).astype(BF16))
        mvs.append(jnp.where(head, mv, jnp.zeros_like(mv)))
    o = jnp.dot(jnp.concatenate(ps, axis=1), jnp.concatenate(mvs, axis=0), preferred_element_type=F32)
    o_ref[...] = o.astype(o_ref.dtype)


def _mem_prompt(z, mkt, mv, layer, seq, tq):
    m = z.shape[0]
    tiles_per_seq = seq // tq
    q_col = MIX_WIDTH // MEM_WIDTH
    kv_spec = pl.BlockSpec((1, 1, MEM_WIDTH, MEM_TOKENS), lambda i: (layer, i // tiles_per_seq, 0, 0))
    return pl.pallas_call(
        _mem_prompt_kernel,
        grid=(m // tq,),
        in_specs=[pl.BlockSpec((tq, MEM_WIDTH), lambda i: (i, q_col)), kv_spec, kv_spec],
        out_specs=pl.BlockSpec((tq, MEM_WIDTH), lambda i: (i, 0)),
        out_shape=jax.ShapeDtypeStruct((m, MEM_WIDTH), BF16),
        compiler_params=_cparams(1),
        name="mem_prompt",
    )(z, mkt, mv)


def _mem_sample_kernel(q_ref, mkt_ref, mvt_ref, o_ref):
    q = q_ref[...]
    mkt = mkt_ref[0].astype(BF16)
    mvt = mvt_ref[0].astype(BF16)
    s = jnp.einsum('bqd,bdm->bqm', q, mkt, preferred_element_type=F32)
    p = _softmax_rows(s).astype(BF16)
    o_ref[...] = jnp.einsum('bqm,bdm->bqd', p, mvt, preferred_element_type=F32)


def _mem_sample(q_rows, cache_kt, cache_vt, layer, bb):
    nb, rows, _ = q_rows.shape
    kv_spec = pl.BlockSpec((1, bb, MEM_WIDTH, MEM_TOKENS), lambda i: (layer, i, 0, 0))
    return pl.pallas_call(
        _mem_sample_kernel,
        grid=(nb // bb,),
        in_specs=[pl.BlockSpec((bb, rows, MEM_WIDTH), lambda i: (i, 0, 0)), kv_spec, kv_spec],
        out_specs=pl.BlockSpec((bb, rows, MEM_WIDTH), lambda i: (i, 0, 0)),
        out_shape=jax.ShapeDtypeStruct((nb, rows, MEM_WIDTH), F32),
        compiler_params=_cparams(1),
        name="mem_sample",
    )(q_rows, cache_kt, cache_vt)


def _subln(o, g_ref, lam_init):
    return _rms_unit(o) * g_ref[...] * (1.0 - lam_init)


def _lane_tiles(x):
    return [x[:, c * LANES:(c + 1) * LANES] for c in range(x.shape[1] // LANES)]


def _diff_prompt_kernel(lq1, lk1, lq2, lk2, g_ref, q_ref, cos_ref, slo_ref, shi_ref, kt_ref, v_ref,
                        o_ref, qm0, qm1, s0, s1, m0, m1, l_ref, acc_ref, *, j, lam_init, tq, nq):
    i = pl.program_id(2)

    def prologue(qm, m):
        q = _rope_slab(q_ref[...], cos_ref[...], slo_ref[...], shi_ref[...]) * (DIFF_HEAD_DIM ** -0.5)
        lane = lax.broadcasted_iota(jnp.int32, (1, LANES), 1)
        qm[0] = jnp.where(lane < DIFF_HEAD_DIM, q, 0.0).astype(BF16)
        qm[1] = jnp.where(lane >= DIFF_HEAD_DIM, q, 0.0).astype(BF16)
        m[...] = jnp.full(m.shape, NEG_INF, F32)

    def scores(qm, s_buf, m, kv, masked):
        kt = kt_ref[0, kv]
        for a in range(2):
            s = jnp.dot(qm[a], kt, preferred_element_type=F32)
            if masked:
                row = lax.broadcasted_iota(jnp.int32, (tq, tq), 0)
                col = lax.broadcasted_iota(jnp.int32, (tq, tq), 1)
                s = jnp.where(col <= row, s, NEG_INF)
            s_buf[a, kv] = s
            m[a] = functools.reduce(jnp.maximum, _lane_tiles(s), m[a])

    def finish_max(m):
        for a in range(2):
            m[a] = jnp.broadcast_to(jnp.max(m[a], axis=-1, keepdims=True), (tq, LANES))

    def clear_sums():
        l_ref[...] = jnp.zeros(l_ref.shape, F32)
        acc_ref[...] = jnp.zeros(acc_ref.shape, F32)

    def weigh(s_buf, m, kv):
        v = v_ref[0, 0, pl.ds(pl.multiple_of(kv * tq, tq), tq), :]
        for a in range(2):
            row_max = m[a]
            p = [jnp.exp(t - row_max) for t in _lane_tiles(s_buf[a, kv])]
            l_ref[a] += functools.reduce(lambda x, y: x + y, p)
            acc_ref[a] += jnp.dot(jnp.concatenate(p, axis=1).astype(BF16), v, preferred_element_type=F32)

    def epilogue():
        lam = _lambda_value(lq1[...], lk1[...], lq2[...], lk2[...], j, lam_init)
        inv = [1.0 / jnp.sum(l_ref[a], axis=-1, keepdims=True) for a in range(2)]
        o = acc_ref[0] * inv[0] - lam * (acc_ref[1] * inv[1])
        o_ref[...] = _subln(o, g_ref, lam_init).astype(o_ref.dtype)

    def middle(qm, s_cur, m_cur, s_prev, m_prev, odd):
        prologue(qm, m_cur)
        clear_sums()

        def both(kv):
            scores(qm, s_cur, m_cur, kv, False)
            weigh(s_prev, m_prev, kv)

        def body(pair, carry):
            both(2 * pair)
            both(2 * pair + 1)
            return carry

        lax.fori_loop(0, lax.shift_right_logical(i, 1), body, 0)
        if odd:
            both(i - 1)
        scores(qm, s_cur, m_cur, i, True)
        finish_max(m_cur)
        epilogue()

    @pl.when(i == 0)
    def _():
        prologue(qm0, m0)
        scores(qm0, s0, m0, 0, True)
        finish_max(m0)

    @pl.when((i > 0) & (i < nq) & (i % 2 == 0))
    def _():
        middle(qm0, s0, m0, s1, m1, False)

    @pl.when((i < nq) & (i % 2 == 1))
    def _():
        middle(qm1, s1, m1, s0, m0, True)

    @pl.when(i == nq)
    def _():
        s_prev, m_prev = (s1, m1) if nq % 2 == 0 else (s0, m0)
        clear_sums()

        def body(pair, carry):
            weigh(s_prev, m_prev, 2 * pair)
            weigh(s_prev, m_prev, 2 * pair + 1)
            return carry

        lax.fori_loop(0, nq // 2, body, 0)
        if nq % 2:
            weigh(s_prev, m_prev, nq - 1)
        epilogue()


def _diff_prompt(z, kt, vb, tabs, lams, g, j, lam_init, batch, seq, tq):
    m = z.shape[0]
    nq = seq // tq
    lam_spec = _const_spec((N_B, DIFF_HEAD_DIM))
    cur = lambda i: jnp.minimum(i, nq - 1)
    done = lambda i: jnp.maximum(i - 1, 0)
    tab_spec = pl.BlockSpec((tq, LANES), lambda b, h, i: (cur(i), 0))
    return pl.pallas_call(
        functools.partial(_diff_prompt_kernel, j=j, lam_init=lam_init, tq=tq, nq=nq),
        grid=(batch, DIFF_HEADS, nq + 1),
        in_specs=[lam_spec, lam_spec, lam_spec, lam_spec,
                  _const_spec((1, DIFF_V_DIM)),
                  pl.BlockSpec((tq, LANES), lambda b, h, i: (b * nq + cur(i), h)),
                  tab_spec, tab_spec, tab_spec,
                  pl.BlockSpec((1, nq, LANES, tq), lambda b, h, i: (b, 0, h, 0)),
                  pl.BlockSpec((1, 1, seq, LANES), lambda b, h, i: (b, h, 0, 0))],
        out_specs=pl.BlockSpec((tq, LANES), lambda b, h, i: (b * nq + done(i), h)),
        out_shape=jax.ShapeDtypeStruct((m, MIX_WIDTH), BF16),
        scratch_shapes=[pltpu.VMEM((2, tq, LANES), BF16), pltpu.VMEM((2, tq, LANES), BF16),
                        pltpu.VMEM((2, nq, tq, tq), F32), pltpu.VMEM((2, nq, tq, tq), F32),
                        pltpu.VMEM((2, tq, LANES), F32), pltpu.VMEM((2, tq, LANES), F32),
                        pltpu.VMEM((2, tq, LANES), F32),
                        pltpu.VMEM((2, tq, LANES), F32)],
        compiler_params=_cparams(3),
        name="diff_prompt",
    )(*lams, g.reshape(1, DIFF_V_DIM), z, *tabs, kt.reshape(batch, nq, MIX_WIDTH, tq), vb)


def _post_kernel(x_ref, mix_ref, mem_ref, woa_ref, wob_ref, g2_ref, w1_ref, w2_ref, *rest, final):
    if final:
        gf_ref, o_ref = rest
    else:
        (o_ref,) = rest
    o_ref[...] = (x_ref[...]
                  + jnp.dot(mix_ref[...], woa_ref[...], preferred_element_type=F32)
                  + jnp.dot(mem_ref[...], wob_ref[...], preferred_element_type=F32))
    h = (_rms_unit(o_ref[...]) * g2_ref[...]).astype(BF16)
    for c in range(D_FF // FF_CHUNK):
        sl = slice(c * FF_CHUNK, (c + 1) * FF_CHUNK)
        a = jnp.maximum(jnp.dot(h, w1_ref[:, sl], preferred_element_type=F32), 0.0)
        o_ref[...] += jnp.dot((a * a).astype(BF16), w2_ref[sl, :], preferred_element_type=F32)
    if final:
        o_ref[...] = _rms_unit(o_ref[...]) * gf_ref[...]


def _post(x, mix, mem, woa, wob, g2, w1, w2, gf, tm):
    m, d = x.shape
    final = gf is not None
    row = lambda n: pl.BlockSpec((tm, n), lambda i: (i, 0))
    in_specs = [row(d), row(MIX_WIDTH), row(MEM_WIDTH),
                _const_spec((MIX_WIDTH, d)), _const_spec((MEM_WIDTH, d)), _const_spec((1, d)),
                _const_spec((d, D_FF)), _const_spec((D_FF, d))]
    args = [x, mix, mem, woa, wob, g2.reshape(1, d), w1, w2]
    if final:
        in_specs.append(_const_spec((1, d)))
        args.append(gf.reshape(1, d))
    return pl.pallas_call(
        functools.partial(_post_kernel, final=final),
        grid=(m // tm,),
        in_specs=in_specs,
        out_specs=row(d),
        out_shape=jax.ShapeDtypeStruct((m, d), F32),
        compiler_params=_cparams(1),
        name="post_final" if final else "post",
    )(*args)


def _post_attn_kernel(pt_ref, x_ref, mix_ref, mem_ref, woa_ref, wob_ref, g2_ref, w1_ref, w2_ref,
                      lq1, lk1, lq2, lk2, g_ref, z_ref, cos_ref, slo_ref, shi_ref, kn_ref, vn_ref,
                      ck_hbm, cv_hbm, o_ref, mixs_ref,
                      kbuf, vbuf, sem, x_scr, pgk, pgv,
                      *, j, lam_init, n_pages, n_tok, per_step, n_req):
    i = pl.program_id(0)
    past = n_pages * PAGE_SIZE
    n_keys = past + PAGE_SIZE
    rows = 2 * n_tok

    def page_copies(req, slot):
        cps = []
        for r in range(n_pages):
            pg = pt_ref[req, r]
            cps.append(pltpu.make_async_copy(ck_hbm.at[pg], kbuf.at[slot, r], sem.at[slot, 0]))
            cps.append(pltpu.make_async_copy(cv_hbm.at[pg], vbuf.at[slot, r], sem.at[slot, 1]))
        return cps

    @pl.when(i == 0)
    def _():
        for u in range(per_step):
            for cp in page_copies(u, u):
                cp.start()

    o_ref[...] = (x_ref[...]
                  + jnp.dot(mix_ref[...], woa_ref[...], preferred_element_type=F32)
                  + jnp.dot(mem_ref[...], wob_ref[...], preferred_element_type=F32))
    hmid = (_rms_unit(o_ref[...]) * g2_ref[...]).astype(BF16)

    lam = _lambda_value(lq1[...], lk1[...], lq2[...], lk2[...], j, lam_init)
    row = lax.broadcasted_iota(jnp.int32, (rows, LANES), 0)
    lane = lax.broadcasted_iota(jnp.int32, (rows, LANES), 1)
    own_map = (row < n_tok) == (lane < DIFF_HEAD_DIM)
    key = lax.broadcasted_iota(jnp.int32, (rows, n_keys), 1)
    tok = lax.broadcasted_iota(jnp.int32, (rows, n_keys), 0) % n_tok
    visible = key - past <= tok
    cos, slo, shi = cos_ref[...], slo_ref[...], shi_ref[...]

    def attention(u):
        x_scr[...] = jnp.zeros(x_scr.shape, F32)
        x_scr[0:n_tok, :] = z_ref[u]
        x = x_scr[...]
        x = x + pltpu.roll(x, n_tok, 0)
        pgk[...] = jnp.zeros(pgk.shape, F32)
        pgk[0:n_tok, :] = kn_ref[u]
        pgv[...] = jnp.zeros(pgv.shape, F32)
        pgv[0:n_tok, :] = vn_ref[u]
        for h in range(DIFF_HEADS):
            sl = slice(h * LANES, (h + 1) * LANES)
            kt = jnp.concatenate([kbuf[u, r, sl, :].astype(BF16) for r in range(n_pages)]
                                 + [pgk[:, sl].T.astype(BF16)], axis=1)
            vv = jnp.concatenate([vbuf[u, r, h].astype(BF16) for r in range(n_pages)]
                                 + [pgv[:, sl].astype(BF16)], axis=0)
            q = _rope_slab(x[:, sl], cos, slo, shi) * (DIFF_HEAD_DIM ** -0.5)
            q = jnp.where(own_map, q, 0.0).astype(BF16)
            s = jnp.dot(q, kt, preferred_element_type=F32)
            p = _softmax_rows(jnp.where(visible, s, NEG_INF))
            a = p - lam * pltpu.roll(p, n_tok, 0)
            o = jnp.dot(a.astype(BF16), vv, preferred_element_type=F32)
            mixs_ref[u, :, sl] = _subln(o[0:n_tok], g_ref, lam_init).astype(mixs_ref.dtype)

    n_chunks = D_FF // FF_CHUNK
    for u in range(per_step):
        req = i * per_step + u
        for cp in page_copies(req, u):
            cp.wait()
        attention(u)
        for c in range(u * n_chunks // per_step, (u + 1) * n_chunks // per_step):
            sl = slice(c * FF_CHUNK, (c + 1) * FF_CHUNK)
            a = jnp.maximum(jnp.dot(hmid, w1_ref[:, sl], preferred_element_type=F32), 0.0)
            o_ref[...] += jnp.dot((a * a).astype(BF16), w2_ref[sl, :], preferred_element_type=F32)
        @pl.when(req + per_step < n_req)
        def _():
            for cp in page_copies(req + per_step, u):
                cp.start()


def _post_attn(x, mix, mem, woa, wob, g2, w1, w2, tm,
               page_table, zb, tabs, k_new, v_new, cache_kt, cache_vh, lams, g, j, lam_init):
    m, d = x.shape
    n_req, n_tok, _ = zb.shape
    n_pages = page_table.shape[1]
    steps = m // tm
    per_step = n_req // steps
    assert per_step * steps == n_req and (D_FF // FF_CHUNK) % per_step == 0
    rows = 2 * n_tok

    def const(shape):
        nd = len(shape)
        return pl.BlockSpec(shape, lambda *_: (0,) * nd, pipeline_mode=pl.Buffered(1))

    row = lambda n: pl.BlockSpec((tm, n), lambda i, pt: (i, 0))
    req_spec = pl.BlockSpec((per_step, n_tok, MIX_WIDTH), lambda i, pt: (i, 0, 0))
    lam_spec = const((N_B, DIFF_HEAD_DIM))
    tab_spec = const((rows, LANES))
    grid_spec = pltpu.PrefetchScalarGridSpec(
        num_scalar_prefetch=1,
        grid=(steps,),
        in_specs=[row(d), row(MIX_WIDTH), row(MEM_WIDTH),
                  const((MIX_WIDTH, d)), const((MEM_WIDTH, d)), const((1, d)),
                  const((d, D_FF)), const((D_FF, d)),
                  lam_spec, lam_spec, lam_spec, lam_spec, const((1, DIFF_V_DIM)),
                  req_spec, tab_spec, tab_spec, tab_spec, req_spec, req_spec,
                  pl.BlockSpec(memory_space=pl.ANY), pl.BlockSpec(memory_space=pl.ANY)],
        out_specs=[row(d), req_spec],
        scratch_shapes=[pltpu.VMEM((per_step, n_pages, MIX_WIDTH, PAGE_SIZE), F32),
                        pltpu.VMEM((per_step, n_pages, DIFF_HEADS, PAGE_SIZE, LANES), F32),
                        pltpu.SemaphoreType.DMA((per_step, 2)),
                        pltpu.VMEM((rows, MIX_WIDTH), F32),
                        pltpu.VMEM((PAGE_SIZE, MIX_WIDTH), F32),
                        pltpu.VMEM((PAGE_SIZE, MIX_WIDTH), F32)],
    )
    return pl.pallas_call(
        functools.partial(_post_attn_kernel, j=j, lam_init=lam_init, n_pages=n_pages, n_tok=n_tok,
                          per_step=per_step, n_req=n_req),
        grid_spec=grid_spec,
        out_shape=[jax.ShapeDtypeStruct((m, d), F32),
                   jax.ShapeDtypeStruct((n_req, n_tok, MIX_WIDTH), BF16)],
        compiler_params=pltpu.CompilerParams(dimension_semantics=("arbitrary",),
                                             vmem_limit_bytes=FUSED_VMEM_LIMIT),
        name="post_attn",
    )(page_table, x, mix, mem, woa, wob, g2.reshape(1, d), w1, w2,
      *lams, g.reshape(1, DIFF_V_DIM), zb, *tabs, k_new, v_new, cache_kt, cache_vh)


def _rope_tables(pos):
    inv = 1.0 / (ROPE_THETA ** (jnp.arange(0, DIFF_HEAD_DIM, 2, dtype=F32) / DIFF_HEAD_DIM))
    ang = pos.astype(F32)[:, None] * inv[None, :]
    cos, sin = jnp.cos(ang), jnp.sin(ang)
    zero = jnp.zeros_like(sin)
    tile2 = lambda a, b: jnp.concatenate([a, b, a, b], axis=-1)
    return tile2(cos, cos), tile2(-sin, zero), tile2(zero, sin)


def _lam_init(layer):
    return 0.8 - 0.6 * math.exp(-0.3 * layer)


def kernel(x_prompt, x_sample, state_pool, cache_mem_k, cache_mem_v, cache_k, cache_v, page_table, mem_prompt,
           norm1, norm2, final_norm, kv_norm, mem_norm, w_in, w_out, w_mem_k, w_mem_v, w_pool, pool_scale,
           w_k, w_v, lambda_q1, lambda_k1, lambda_q2, lambda_k2, subln, w_mlp1, w_mlp2):
    bp, tp, d = x_prompt.shape
    bs, ts, _ = x_sample.shape
    n_pages = page_table.shape[1]
    assert d == D_MODEL and n_pages * PAGE_SIZE == PAST_LEN and PAST_LEN >= POOL_HIST

    w_in_b = w_in.astype(BF16)
    woa_b = w_out[:, :MIX_WIDTH].astype(BF16)
    wob_b = w_out[:, MIX_WIDTH:].astype(BF16)
    w1_b = w_mlp1.astype(BF16)
    w2_b = w_mlp2.astype(BF16)
    wk_b = w_k.astype(BF16)
    wv_b = w_v.astype(BF16)
    w_memkv = jnp.concatenate([w_mem_k, w_mem_v], axis=-1)
    w_memkv = w_memkv.transpose(1, 0, 2).reshape(d, DEPTH * 2 * MEM_WIDTH).astype(BF16)
    wbd = jnp.zeros((N_A, MIX_WIDTH, MIX_WIDTH), F32)
    for g in range(len(POOL_WINDOWS)):
        sl = slice(g * GROUP_WIDTH, (g + 1) * GROUP_WIDTH)
        wbd = wbd.at[:, sl, sl].set(w_pool[:, g])
    wbd = wbd.astype(BF16)
    lams = (lambda_q1, lambda_k1, lambda_q2, lambda_k2)

    ms = bs * ts
    pos_s = PAST_LEN + jnp.arange(ts, dtype=jnp.int32)
    tabs_s = _rope_tables(jnp.repeat(pos_s, bs))
    tabs_s2 = _rope_tables(jnp.tile(pos_s, 2))
    ck_t = cache_k.transpose(0, 2, 3, 4, 1).reshape(cache_k.shape[0], MIX_WIDTH, PAGE_SIZE)
    cv_h = cache_v.transpose(0, 2, 1, 3)
    cmk_t = cache_mem_k.transpose(0, 1, 3, 4, 2).reshape(DEPTH, bs, MEM_WIDTH, MEM_TOKENS)
    cmv_t = cache_mem_v.transpose(0, 1, 3, 4, 2).reshape(DEPTH, bs, MEM_WIDTH, MEM_TOKENS)
    hist = state_pool.transpose(0, 2, 1, 3)
    head_of_col = jnp.arange(MEM_WIDTH) // MEM_HEAD_DIM
    head_mask = (head_of_col[None, :] == jnp.arange(MEM_HEADS)[:, None]).astype(F32)
    to_requests = lambda a: a.reshape(ts, bs, -1).transpose(1, 0, 2)
    kv_s = {}

    def sample_tail(l, x, z, mix):
        qm = to_requests(z[:, MIX_WIDTH:])[:, None] * (MEM_HEAD_DIM ** -0.5) * head_mask[None, :, None, :]
        om = _mem_sample(qm.reshape(bs, MEM_HEADS * ts, MEM_WIDTH).astype(BF16), cmk_t, cmv_t, l, 16)
        mem = jnp.sum(om.reshape(bs, MEM_HEADS, ts, MEM_WIDTH) * head_mask[None, :, None, :], axis=1)
        mem = mem.transpose(1, 0, 2).reshape(ms, MEM_WIDTH).astype(BF16)
        gf = final_norm if l == DEPTH - 1 else None
        x = _post(x, mix, mem, woa_b[l], wob_b[l], norm2[l], w1_b[l], w2_b[l], gf, ms // 2)
        if l + 1 == N_A:
            kv_s['kt'], kv_s['vh'], z, kv_s['k'], kv_s['v'] = _kv_proj(
                x, kv_norm, norm1[l + 1], wk_b, wv_b, w_in_b[l + 1], tabs_s, ts, bs, bs, False)
        elif l + 1 < DEPTH:
            z = _norm_mm(x, norm1[l + 1], w_in_b[l + 1], ms)
        return x, z

    xs = x_sample.transpose(1, 0, 2).reshape(ms, d)
    pool_sample = []
    zs = _norm_mm(xs, norm1[0], w_in_b[0], ms)
    for l in range(N_A):
        mix_s, kept = _pool_sample(hist, zs.reshape(ts, bs, d), l, wbd[l], pool_scale[l], 32)
        pool_sample.append(kept)
        xs, zs = sample_tail(l, xs, zs, mix_s.reshape(ms, MIX_WIDTH))
    k_new, v_new = to_requests(kv_s['k']), to_requests(kv_s['v'])

    assert N_B < DEPTH
    tm = 512
    mp = bp * tp
    mem_kt, mem_vt, mem_kt_b, mem_v_b = _mem_proj(mem_prompt.reshape(bp * MEM_TOKENS, d), mem_norm, w_memkv, bp)
    to_heads = lambda a: a.reshape(DEPTH, bp, MEM_HEADS, MEM_HEAD_DIM, MEM_TOKENS).transpose(0, 1, 4, 2, 3)
    mem_k_prompt, mem_v_prompt = to_heads(mem_kt), to_heads(mem_vt)

    tabs_p = _rope_tables(jnp.arange(tp, dtype=jnp.int32))
    x = x_prompt.reshape(mp, d)
    pool_prompt = []
    z = _norm_mm(x, norm1[0], w_in_b[0], tm)
    for l in range(DEPTH):
        if l < N_A:
            pool_prompt.append(z.reshape(bp, tp, d)[:, tp - POOL_HIST:, :MIX_WIDTH])
            mix = _pool_prompt(z, wbd[l], pool_scale[l], tp, tm)
        else:
            j = l - N_A
            mix = _diff_prompt(z, kt_b, vh_b, tabs_p, lams, subln[j], j, _lam_init(l), bp, tp, tm)
        mem = _mem_prompt(z, mem_kt_b, mem_v_b, l, tp, tm)
        if l < N_B:
            ls = N_A + l
            x, mix_s = _post_attn(x, mix, mem, woa_b[l], wob_b[l], norm2[l], w1_b[l], w2_b[l], tm // 2,
                                  page_table, to_requests(zs[:, :MIX_WIDTH]), tabs_s2, k_new, v_new,
                                  ck_t, cv_h, lams, subln[l], l, _lam_init(ls))
            xs, zs = sample_tail(ls, xs, zs, mix_s.transpose(1, 0, 2).reshape(ms, MIX_WIDTH))
        else:
            gf = final_norm if l == DEPTH - 1 else None
            x = _post(x, mix, mem, woa_b[l], wob_b[l], norm2[l], w1_b[l], w2_b[l], gf, tm)
        if l + 1 == N_A:
            kt_p, vh_p, z, kt_b, vh_b = _kv_proj(x, kv_norm, norm1[l + 1], wk_b, wv_b, w_in_b[l + 1],
                                                 tabs_p, bp, tp, tm, True)
        elif l + 1 < DEPTH:
            z = _norm_mm(x, norm1[l + 1], w_in_b[l + 1], tm)
    y_prompt = x.reshape(bp, tp, d)
    pool_prompt = jnp.stack(pool_prompt, axis=0)
    k_prompt = kt_p.reshape(bp, DIFF_HEADS, 2, DIFF_HEAD_DIM, tp).transpose(0, 4, 1, 2, 3)
    v_prompt = vh_p.transpose(0, 2, 1, 3)

    y_sample = xs.reshape(ts, bs, d).transpose(1, 0, 2)
    pool_sample = jnp.stack(pool_sample, axis=0).transpose(0, 2, 1, 3)
    k_sample = kv_s['kt'].reshape(ts, DIFF_HEADS, 2, DIFF_HEAD_DIM, bs).transpose(4, 0, 1, 2, 3)
    v_sample = kv_s['vh'].transpose(2, 0, 1, 3)

    return (y_prompt, y_sample, pool_prompt, mem_k_prompt, mem_v_prompt, k_prompt, v_prompt,
            pool_sample, k_sample, v_sample)
```

```python
import functools
import math

import jax
import jax.numpy as jnp
from jax import lax
from jax.experimental import pallas as pl
from jax.experimental.pallas import tpu as pltpu

D_MODEL = 1024
DEPTH = 4
N_A = DEPTH // 2
N_B = DEPTH - N_A
PAST_LEN = 2048
PAGE_SIZE = 128
MEM_TOKENS = 256
MEM_HEADS = 4
MEM_HEAD_DIM = D_MODEL // 16
MEM_WIDTH = MEM_HEADS * MEM_HEAD_DIM
MIX_WIDTH = D_MODEL - MEM_WIDTH
POOL_WINDOWS = (2, 4, 8, 16)
GROUP_WIDTH = MIX_WIDTH // len(POOL_WINDOWS)
POOL_HIST = max(POOL_WINDOWS) - 1
POOL_PAD = POOL_HIST + 1
DIFF_HEAD_DIM = 64
DIFF_HEADS = MIX_WIDTH // (2 * DIFF_HEAD_DIM)
DIFF_V_DIM = 2 * DIFF_HEAD_DIM
D_FF = 4 * D_MODEL
ROPE_THETA = 10000.0
RMS_EPS = 1e-6
NEG_INF = -1e30

LANES = 128
FF_CHUNK = 1024
FF_PIECE = 1024
VMEM_LIMIT = 56 * 1024 * 1024
FUSED_VMEM_LIMIT = 60 * 1024 * 1024

BF16 = jnp.bfloat16
F32 = jnp.float32


def _cparams(n_axes):
    return pltpu.CompilerParams(dimension_semantics=("arbitrary",) * n_axes,
                                vmem_limit_bytes=VMEM_LIMIT)


def _const_spec(shape):
    nd = len(shape)
    return pl.BlockSpec(shape, lambda *_: (0,) * nd)


def _rms_unit(x):
    return x * lax.rsqrt(jnp.mean(x * x, axis=-1, keepdims=True) + RMS_EPS)


def _rope_slab(x, cos, sin_lo, sin_hi):
    return (x * cos + pltpu.roll(x, LANES - DIFF_HEAD_DIM // 2, 1) * sin_lo
            + pltpu.roll(x, DIFF_HEAD_DIM // 2, 1) * sin_hi)


def _lambda_value(lq1, lk1, lq2, lk2, j, lam_init):
    a = jnp.sum(lq1[j:j + 1, :] * lk1[j:j + 1, :], axis=-1, keepdims=True)
    b = jnp.sum(lq2[j:j + 1, :] * lk2[j:j + 1, :], axis=-1, keepdims=True)
    return jnp.exp(a) - jnp.exp(b) + lam_init


def _group_select(lane, s2, s4, s8, s16):
    return jnp.where(lane < GROUP_WIDTH, s2 * (1.0 / POOL_WINDOWS[0]),
                     jnp.where(lane < 2 * GROUP_WIDTH, s4 * (1.0 / POOL_WINDOWS[1]),
                               jnp.where(lane < 3 * GROUP_WIDTH, s8 * (1.0 / POOL_WINDOWS[2]),
                                         s16 * (1.0 / POOL_WINDOWS[3]))))


def _norm_mm_kernel(x_ref, g_ref, w_ref, o_ref):
    h = _rms_unit(x_ref[...]) * g_ref[...]
    o_ref[...] = jnp.dot(h.astype(BF16), w_ref[...], preferred_element_type=F32)


def _norm_mm(x, g, w, tm):
    m, d = x.shape
    n = w.shape[1]
    return pl.pallas_call(
        _norm_mm_kernel,
        grid=(m // tm,),
        in_specs=[pl.BlockSpec((tm, d), lambda i: (i, 0)),
                  _const_spec((1, d)),
                  _const_spec((d, n))],
        out_specs=pl.BlockSpec((tm, n), lambda i: (i, 0)),
        out_shape=jax.ShapeDtypeStruct((m, n), F32),
        compiler_params=_cparams(1),
        name="norm_mm",
    )(x, g.reshape(1, d), w)


def _mem_proj_kernel(x_ref, g_ref, w_ref, kt_ref, vt_ref, ktb_ref, vb_ref):
    h = (_rms_unit(x_ref[...]) * g_ref[...]).astype(BF16)
    y = jnp.dot(h, w_ref[...], preferred_element_type=F32)
    for l in range(DEPTH):
        yk = y[:, (2 * l) * MEM_WIDTH:(2 * l + 1) * MEM_WIDTH]
        yv = y[:, (2 * l + 1) * MEM_WIDTH:(2 * l + 2) * MEM_WIDTH]
        ykt = yk.T
        kt_ref[l, 0] = ykt
        ktb_ref[l, 0] = ykt.astype(BF16)
        vt_ref[l, 0] = yv.T
        vb_ref[l, 0] = yv.astype(BF16)


def _mem_proj(x, g, w, batch):
    d = x.shape[1]
    spec = pl.BlockSpec((DEPTH, 1, MEM_WIDTH, MEM_TOKENS), lambda b: (0, b, 0, 0))
    spec_n = pl.BlockSpec((DEPTH, 1, MEM_TOKENS, MEM_WIDTH), lambda b: (0, b, 0, 0))
    return pl.pallas_call(
        _mem_proj_kernel,
        grid=(batch,),
        in_specs=[pl.BlockSpec((MEM_TOKENS, d), lambda b: (b, 0)),
                  _const_spec((1, d)),
                  _const_spec((d, w.shape[1]))],
        out_specs=[spec, spec, spec, spec_n],
        out_shape=[jax.ShapeDtypeStruct((DEPTH, batch, MEM_WIDTH, MEM_TOKENS), F32),
                   jax.ShapeDtypeStruct((DEPTH, batch, MEM_WIDTH, MEM_TOKENS), F32),
                   jax.ShapeDtypeStruct((DEPTH, batch, MEM_WIDTH, MEM_TOKENS), BF16),
                   jax.ShapeDtypeStruct((DEPTH, batch, MEM_TOKENS, MEM_WIDTH), BF16)],
        compiler_params=_cparams(1),
        name="mem_proj",
    )(x, g.reshape(1, d), w)


def _kv_kernel(x_ref, gkv_ref, gn_ref, wk_ref, wv_ref, win_ref, cos_ref, slo_ref, shi_ref,
               kt_ref, vh_ref, z_ref, a_ref, b_ref, *, attention_copies):
    y = _rms_unit(x_ref[...])
    hk = (y * gkv_ref[...]).astype(BF16)
    hn = (y * gn_ref[...]).astype(BF16)
    z_ref[...] = jnp.dot(hn, win_ref[...], preferred_element_type=F32)
    v = jnp.dot(hk, wv_ref[...], preferred_element_type=F32)
    kraw = jnp.dot(hk, wk_ref[...], preferred_element_type=F32)
    cos, slo, shi = cos_ref[...], slo_ref[...], shi_ref[...]
    if not attention_copies:
        b_ref[...] = v
    for h in range(DIFF_HEADS):
        sl = slice(h * LANES, (h + 1) * LANES)
        kr = _rope_slab(kraw[:, sl], cos, slo, shi)
        krt = kr.T
        kt_ref[0, sl, :] = krt
        vh_ref[0, h] = v[:, sl]
        if attention_copies:
            a_ref[0, sl, :] = krt.astype(BF16)
            b_ref[0, h] = v[:, sl].astype(BF16)
        else:
            a_ref[:, sl] = kr


def _kv_proj(x, gkv, gn, wk, wv, win, tabs, n_seq, seq, tm, attention_copies):
    m, d = x.shape
    tps = seq // tm
    n_tab = tabs[0].shape[0] // tm
    tab_spec = pl.BlockSpec((tm, LANES), lambda i: (i % n_tab, 0))
    row = lambda n: pl.BlockSpec((tm, n), lambda i: (i, 0))
    kt_spec = pl.BlockSpec((1, MIX_WIDTH, tm), lambda i: (i // tps, 0, i % tps))
    vh_spec = pl.BlockSpec((1, DIFF_HEADS, tm, LANES), lambda i: (i // tps, 0, i % tps, 0))
    out_specs = [kt_spec, vh_spec, row(D_MODEL)]
    out_shape = [jax.ShapeDtypeStruct((n_seq, MIX_WIDTH, seq), F32),
                 jax.ShapeDtypeStruct((n_seq, DIFF_HEADS, seq, LANES), F32),
                 jax.ShapeDtypeStruct((m, D_MODEL), F32)]
    if attention_copies:
        out_specs += [pl.BlockSpec((1, MIX_WIDTH, tm), lambda i: (i, 0, 0)), vh_spec]
        out_shape += [jax.ShapeDtypeStruct((m // tm, MIX_WIDTH, tm), BF16),
                      jax.ShapeDtypeStruct((n_seq, DIFF_HEADS, seq, LANES), BF16)]
    else:
        out_specs += [row(MIX_WIDTH), row(MIX_WIDTH)]
        out_shape += [jax.ShapeDtypeStruct((m, MIX_WIDTH), F32)] * 2
    return pl.pallas_call(
        functools.partial(_kv_kernel, attention_copies=attention_copies),
        grid=(m // tm,),
        in_specs=[row(d), _const_spec((1, d)), _const_spec((1, d)),
                  _const_spec((d, MIX_WIDTH)), _const_spec((d, MIX_WIDTH)), _const_spec((d, D_MODEL)),
                  tab_spec, tab_spec, tab_spec],
        out_specs=out_specs,
        out_shape=out_shape,
        compiler_params=_cparams(1),
        name="kv_proj",
    )(x, gkv.reshape(1, d), gn.reshape(1, d), wk, wv, win, *tabs)


def _pool_prompt_kernel(u_ref, h_ref, w_ref, scale_ref, o_ref, *, tiles_per_seq, tm):
    t0 = (pl.program_id(0) % tiles_per_seq) * tm
    hist = jnp.where(t0 > 0, h_ref[...], 0.0)
    ext = jnp.concatenate([hist, u_ref[...]], axis=0)
    s2 = ext + pltpu.roll(ext, 1, 0)
    s4 = s2 + pltpu.roll(s2, 2, 0)
    s8 = s4 + pltpu.roll(s4, 4, 0)
    s16 = s8 + pltpu.roll(s8, 8, 0)
    pos = t0 - POOL_PAD + lax.broadcasted_iota(jnp.int32, (tm + POOL_PAD, 1), 0)
    inv = [1.0 / jnp.clip(pos + 1, 1, w).astype(F32) for w in POOL_WINDOWS]
    lane = lax.broadcasted_iota(jnp.int32, (1, MIX_WIDTH), 1)
    mean = jnp.where(lane < GROUP_WIDTH, s2 * inv[0],
                     jnp.where(lane < 2 * GROUP_WIDTH, s4 * inv[1],
                               jnp.where(lane < 3 * GROUP_WIDTH, s8 * inv[2], s16 * inv[3])))
    pooled = (mean - ext)[POOL_PAD:]
    mix = jnp.dot(pooled.astype(BF16), w_ref[...], preferred_element_type=F32) * scale_ref[...]
    o_ref[...] = mix.astype(o_ref.dtype)


def _pool_prompt(z, wbd, scale, seq, tm):
    m = z.shape[0]
    hist_blocks = tm // POOL_PAD
    return pl.pallas_call(
        functools.partial(_pool_prompt_kernel, tiles_per_seq=seq // tm, tm=tm),
        grid=(m // tm,),
        in_specs=[pl.BlockSpec((tm, MIX_WIDTH), lambda i: (i, 0)),
                  pl.BlockSpec((POOL_PAD, MIX_WIDTH), lambda i: (jnp.maximum(i * hist_blocks - 1, 0), 0)),
                  _const_spec((MIX_WIDTH, MIX_WIDTH)),
                  _const_spec((1, MIX_WIDTH))],
        out_specs=pl.BlockSpec((tm, MIX_WIDTH), lambda i: (i, 0)),
        out_shape=jax.ShapeDtypeStruct((m, MIX_WIDTH), BF16),
        compiler_params=_cparams(1),
        name="pool_prompt",
    )(z, z, wbd, scale.reshape(1, MIX_WIDTH))


def _pool_sample_kernel(hist_ref, u_ref, w_ref, scale_ref, mix_ref, hist_out_ref, *, n_tok):
    rows = [hist_ref[0, t] for t in range(POOL_HIST)] + [u_ref[t] for t in range(n_tok)]
    lane = lax.broadcasted_iota(jnp.int32, (1, MIX_WIDTH), 1)
    pooled = []
    for t in range(n_tok):
        c = POOL_HIST + t
        back = lambda a, b: functools.reduce(lambda x, y: x + y, [rows[c - i] for i in range(a, b)])
        s2 = back(0, 2)
        s4 = s2 + back(2, 4)
        s8 = s4 + back(4, 8)
        s16 = s8 + back(8, 16)
        pooled.append((_group_select(lane, s2, s4, s8, s16) - rows[c]).astype(BF16))
    nb = pooled[0].shape[0]
    mix = jnp.dot(jnp.concatenate(pooled, axis=0), w_ref[...], preferred_element_type=F32) * scale_ref[...]
    for t in range(n_tok):
        mix_ref[t] = mix[t * nb:(t + 1) * nb].astype(mix_ref.dtype)
    for t in range(POOL_HIST):
        hist_out_ref[t] = rows[t + n_tok]


def _pool_sample(hist, z3, layer, wbd, scale, bb):
    n_tok, nb, _ = z3.shape
    return pl.pallas_call(
        functools.partial(_pool_sample_kernel, n_tok=n_tok),
        grid=(nb // bb,),
        in_specs=[pl.BlockSpec((1, POOL_HIST, bb, MIX_WIDTH), lambda i: (layer, 0, i, 0)),
                  pl.BlockSpec((n_tok, bb, MIX_WIDTH), lambda i: (0, i, 0)),
                  _const_spec((MIX_WIDTH, MIX_WIDTH)),
                  _const_spec((1, MIX_WIDTH))],
        out_specs=[pl.BlockSpec((n_tok, bb, MIX_WIDTH), lambda i: (0, i, 0)),
                   pl.BlockSpec((POOL_HIST, bb, MIX_WIDTH), lambda i: (0, i, 0))],
        out_shape=[jax.ShapeDtypeStruct((n_tok, nb, MIX_WIDTH), BF16),
                   jax.ShapeDtypeStruct((POOL_HIST, nb, MIX_WIDTH), F32)],
        compiler_params=_cparams(1),
        name="pool_sample",
    )(hist, z3, wbd, scale.reshape(1, MIX_WIDTH))


def _softmax_rows(s):
    e = jnp.exp(s - jnp.max(s, axis=-1, keepdims=True))
    return e * (1.0 / jnp.sum(e, axis=-1, keepdims=True))


def _mem_prompt_kernel(q_ref, mkt_ref, mv_ref, o_ref):
    q = q_ref[...] * (MEM_HEAD_DIM ** -0.5)
    mkt = mkt_ref[0, 0]
    mv = mv_ref[0, 0]
    lane = lax.broadcasted_iota(jnp.int32, (1, MEM_WIDTH), 1)
    ps, mvs = [], []
    for h in range(MEM_HEADS):
        head = (lane >= h * MEM_HEAD_DIM) & (lane < (h + 1) * MEM_HEAD_DIM)
        qh = jnp.where(head, q, 0.0).astype(BF16)
        s = jnp.dot(qh, mkt, preferred_element_type=F32)
        ps.append(_softmax_rows(s).astype(BF16))
        mvs.append(jnp.where(head, mv, jnp.zeros_like(mv)))
    o = jnp.dot(jnp.concatenate(ps, axis=1), jnp.concatenate(mvs, axis=0), preferred_element_type=F32)
    o_ref[...] = o.astype(o_ref.dtype)


def _mem_prompt(z, mkt, mv, layer, seq, tq):
    m = z.shape[0]
    tiles_per_seq = seq // tq
    q_col = MIX_WIDTH // MEM_WIDTH
    kv_spec = pl.BlockSpec((1, 1, MEM_WIDTH, MEM_TOKENS), lambda i: (layer, i // tiles_per_seq, 0, 0))
    return pl.pallas_call(
        _mem_prompt_kernel,
        grid=(m // tq,),
        in_specs=[pl.BlockSpec((tq, MEM_WIDTH), lambda i: (i, q_col)), kv_spec, kv_spec],
        out_specs=pl.BlockSpec((tq, MEM_WIDTH), lambda i: (i, 0)),
        out_shape=jax.ShapeDtypeStruct((m, MEM_WIDTH), BF16),
        compiler_params=_cparams(1),
        name="mem_prompt",
    )(z, mkt, mv)


def _mem_sample_kernel(q_ref, mkt_ref, mvt_ref, o_ref):
    q = q_ref[...]
    mkt = mkt_ref[0].astype(BF16)
    mvt = mvt_ref[0].astype(BF16)
    s = jnp.einsum('bqd,bdm->bqm', q, mkt, preferred_element_type=F32)
    p = _softmax_rows(s).astype(BF16)
    o_ref[...] = jnp.einsum('bqm,bdm->bqd', p, mvt, preferred_element_type=F32)


def _mem_sample(q_rows, cache_kt, cache_vt, layer, bb):
    nb, rows, _ = q_rows.shape
    kv_spec = pl.BlockSpec((1, bb, MEM_WIDTH, MEM_TOKENS), lambda i: (layer, i, 0, 0))
    return pl.pallas_call(
        _mem_sample_kernel,
        grid=(nb // bb,),
        in_specs=[pl.BlockSpec((bb, rows, MEM_WIDTH), lambda i: (i, 0, 0)), kv_spec, kv_spec],
        out_specs=pl.BlockSpec((bb, rows, MEM_WIDTH), lambda i: (i, 0, 0)),
        out_shape=jax.ShapeDtypeStruct((nb, rows, MEM_WIDTH), F32),
        compiler_params=_cparams(1),
        name="mem_sample",
    )(q_rows, cache_kt, cache_vt)


def _subln(o, g_ref, lam_init):
    return _rms_unit(o) * g_ref[...] * (1.0 - lam_init)


def _lane_tiles(x):
    return [x[:, c * LANES:(c + 1) * LANES] for c in range(x.shape[1] // LANES)]


def _diff_prompt_kernel(lq1, lk1, lq2, lk2, g_ref, q_ref, cos_ref, slo_ref, shi_ref, kt_ref, v_ref,
                        o_ref, qm0, qm1, s0, s1, m0, m1, l_ref, acc_ref, *, j, lam_init, tq, nq):
    i = pl.program_id(2)

    def prologue(qm, m):
        q = _rope_slab(q_ref[...], cos_ref[...], slo_ref[...], shi_ref[...]) * (DIFF_HEAD_DIM ** -0.5)
        lane = lax.broadcasted_iota(jnp.int32, (1, LANES), 1)
        qm[0] = jnp.where(lane < DIFF_HEAD_DIM, q, 0.0).astype(BF16)
        qm[1] = jnp.where(lane >= DIFF_HEAD_DIM, q, 0.0).astype(BF16)
        m[...] = jnp.full(m.shape, NEG_INF, F32)

    def scores(qm, s_buf, m, kv, masked):
        kt = kt_ref[0, kv]
        for a in range(2):
            s = jnp.dot(qm[a], kt, preferred_element_type=F32)
            if masked:
                row = lax.broadcasted_iota(jnp.int32, (tq, tq), 0)
                col = lax.broadcasted_iota(jnp.int32, (tq, tq), 1)
                s = jnp.where(col <= row, s, NEG_INF)
            s_buf[a, kv] = s
            m[a] = functools.reduce(jnp.maximum, _lane_tiles(s), m[a])

    def finish_max(m):
        for a in range(2):
            m[a] = jnp.broadcast_to(jnp.max(m[a], axis=-1, keepdims=True), (tq, LANES))

    def clear_sums():
        l_ref[...] = jnp.zeros(l_ref.shape, F32)
        acc_ref[...] = jnp.zeros(acc_ref.shape, F32)

    def weigh(s_buf, m, kv):
        v = v_ref[0, 0, pl.ds(pl.multiple_of(kv * tq, tq), tq), :]
        for a in range(2):
            row_max = m[a]
            p = [jnp.exp(t - row_max) for t in _lane_tiles(s_buf[a, kv])]
            l_ref[a] += functools.reduce(lambda x, y: x + y, p)
            acc_ref[a] += jnp.dot(jnp.concatenate(p, axis=1).astype(BF16), v, preferred_element_type=F32)

    def epilogue():
        lam = _lambda_value(lq1[...], lk1[...], lq2[...], lk2[...], j, lam_init)
        inv = [1.0 / jnp.sum(l_ref[a], axis=-1, keepdims=True) for a in range(2)]
        o = acc_ref[0] * inv[0] - lam * (acc_ref[1] * inv[1])
        o_ref[...] = _subln(o, g_ref, lam_init).astype(o_ref.dtype)

    def middle(qm, s_cur, m_cur, s_prev, m_prev, odd):
        prologue(qm, m_cur)
        clear_sums()

        def both(kv):
            scores(qm, s_cur, m_cur, kv, False)
            weigh(s_prev, m_prev, kv)

        def body(pair, carry):
            both(2 * pair)
            both(2 * pair + 1)
            return carry

        lax.fori_loop(0, lax.shift_right_logical(i, 1), body, 0)
        if odd:
            both(i - 1)
        scores(qm, s_cur, m_cur, i, True)
        finish_max(m_cur)
        epilogue()

    @pl.when(i == 0)
    def _():
        prologue(qm0, m0)
        scores(qm0, s0, m0, 0, True)
        finish_max(m0)

    @pl.when((i > 0) & (i < nq) & (i % 2 == 0))
    def _():
        middle(qm0, s0, m0, s1, m1, False)

    @pl.when((i < nq) & (i % 2 == 1))
    def _():
        middle(qm1, s1, m1, s0, m0, True)

    @pl.when(i == nq)
    def _():
        s_prev, m_prev = (s1, m1) if nq % 2 == 0 else (s0, m0)
        clear_sums()

        def body(pair, carry):
            weigh(s_prev, m_prev, 2 * pair)
            weigh(s_prev, m_prev, 2 * pair + 1)
            return carry

        lax.fori_loop(0, nq // 2, body, 0)
        if nq % 2:
            weigh(s_prev, m_prev, nq - 1)
        epilogue()


def _diff_prompt(z, kt, vb, tabs, lams, g, j, lam_init, batch, seq, tq):
    m = z.shape[0]
    nq = seq // tq
    lam_spec = _const_spec((N_B, DIFF_HEAD_DIM))
    cur = lambda i: jnp.minimum(i, nq - 1)
    done = lambda i: jnp.maximum(i - 1, 0)
    tab_spec = pl.BlockSpec((tq, LANES), lambda b, h, i: (cur(i), 0))
    return pl.pallas_call(
        functools.partial(_diff_prompt_kernel, j=j, lam_init=lam_init, tq=tq, nq=nq),
        grid=(batch, DIFF_HEADS, nq + 1),
        in_specs=[lam_spec, lam_spec, lam_spec, lam_spec,
                  _const_spec((1, DIFF_V_DIM)),
                  pl.BlockSpec((tq, LANES), lambda b, h, i: (b * nq + cur(i), h)),
                  tab_spec, tab_spec, tab_spec,
                  pl.BlockSpec((1, nq, LANES, tq), lambda b, h, i: (b, 0, h, 0)),
                  pl.BlockSpec((1, 1, seq, LANES), lambda b, h, i: (b, h, 0, 0))],
        out_specs=pl.BlockSpec((tq, LANES), lambda b, h, i: (b * nq + done(i), h)),
        out_shape=jax.ShapeDtypeStruct((m, MIX_WIDTH), BF16),
        scratch_shapes=[pltpu.VMEM((2, tq, LANES), BF16), pltpu.VMEM((2, tq, LANES), BF16),
                        pltpu.VMEM((2, nq, tq, tq), F32), pltpu.VMEM((2, nq, tq, tq), F32),
                        pltpu.VMEM((2, tq, LANES), F32), pltpu.VMEM((2, tq, LANES), F32),
                        pltpu.VMEM((2, tq, LANES), F32),
                        pltpu.VMEM((2, tq, LANES), F32)],
        compiler_params=_cparams(3),
        name="diff_prompt",
    )(*lams, g.reshape(1, DIFF_V_DIM), z, *tabs, kt.reshape(batch, nq, MIX_WIDTH, tq), vb)


def _post_kernel(x_ref, mix_ref, mem_ref, woa_ref, wob_ref, g2_ref, w1_ref, w2_ref, *rest, final):
    if final:
        gf_ref, o_ref = rest
    else:
        (o_ref,) = rest
    o_ref[...] = (x_ref[...]
                  + jnp.dot(mix_ref[...], woa_ref[...], preferred_element_type=F32)
                  + jnp.dot(mem_ref[...], wob_ref[...], preferred_element_type=F32))
    h = (_rms_unit(o_ref[...]) * g2_ref[...]).astype(BF16)
    for c in range(D_FF // FF_CHUNK):
        sl = slice(c * FF_CHUNK, (c + 1) * FF_CHUNK)
        a = jnp.maximum(jnp.dot(h, w1_ref[:, sl], preferred_element_type=F32), 0.0)
        o_ref[...] += jnp.dot((a * a).astype(BF16), w2_ref[sl, :], preferred_element_type=F32)
    if final:
        o_ref[...] = _rms_unit(o_ref[...]) * gf_ref[...]


def _post(x, mix, mem, woa, wob, g2, w1, w2, gf, tm):
    m, d = x.shape
    final = gf is not None
    row = lambda n: pl.BlockSpec((tm, n), lambda i: (i, 0))
    in_specs = [row(d), row(MIX_WIDTH), row(MEM_WIDTH),
                _const_spec((MIX_WIDTH, d)), _const_spec((MEM_WIDTH, d)), _const_spec((1, d)),
                _const_spec((d, D_FF)), _const_spec((D_FF, d))]
    args = [x, mix, mem, woa, wob, g2.reshape(1, d), w1, w2]
    if final:
        in_specs.append(_const_spec((1, d)))
        args.append(gf.reshape(1, d))
    return pl.pallas_call(
        functools.partial(_post_kernel, final=final),
        grid=(m // tm,),
        in_specs=in_specs,
        out_specs=row(d),
        out_shape=jax.ShapeDtypeStruct((m, d), F32),
        compiler_params=_cparams(1),
        name="post_final" if final else "post",
    )(*args)


def _post_attn_kernel(pt_ref, x_ref, mix_ref, mem_ref, woa_ref, wob_ref, g2_ref, w1_ref, w2_ref,
                      lq1, lk1, lq2, lk2, g_ref, z_ref, cos_ref, slo_ref, shi_ref, kn_ref, vn_ref,
                      ck_hbm, cv_hbm, o_ref, mixs_ref,
                      kbuf, vbuf, sem, x_scr, pgk, pgv,
                      *, j, lam_init, n_pages, n_tok, per_step, n_req):
    i = pl.program_id(0)
    past = n_pages * PAGE_SIZE
    n_keys = past + PAGE_SIZE
    rows = 2 * n_tok

    def page_copies(req, slot):
        cps = []
        for r in range(n_pages):
            pg = pt_ref[req, r]
            cps.append(pltpu.make_async_copy(ck_hbm.at[pg], kbuf.at[slot, r], sem.at[slot, 0]))
            cps.append(pltpu.make_async_copy(cv_hbm.at[pg], vbuf.at[slot, r], sem.at[slot, 1]))
        return cps

    @pl.when(i == 0)
    def _():
        for u in range(per_step):
            for cp in page_copies(u, u):
                cp.start()

    o_ref[...] = (x_ref[...]
                  + jnp.dot(mix_ref[...], woa_ref[...], preferred_element_type=F32)
                  + jnp.dot(mem_ref[...], wob_ref[...], preferred_element_type=F32))
    hmid = (_rms_unit(o_ref[...]) * g2_ref[...]).astype(BF16)

    lam = _lambda_value(lq1[...], lk1[...], lq2[...], lk2[...], j, lam_init)
    row = lax.broadcasted_iota(jnp.int32, (rows, LANES), 0)
    lane = lax.broadcasted_iota(jnp.int32, (rows, LANES), 1)
    own_map = (row < n_tok) == (lane < DIFF_HEAD_DIM)
    key = lax.broadcasted_iota(jnp.int32, (rows, n_keys), 1)
    tok = lax.broadcasted_iota(jnp.int32, (rows, n_keys), 0) % n_tok
    visible = key - past <= tok
    cos, slo, shi = cos_ref[...], slo_ref[...], shi_ref[...]

    def attention_inputs(u):
        x_scr[...] = jnp.zeros(x_scr.shape, F32)
        x_scr[0:n_tok, :] = z_ref[u]
        x = x_scr[...]
        pgk[...] = jnp.zeros(pgk.shape, F32)
        pgk[0:n_tok, :] = kn_ref[u]
        pgv[...] = jnp.zeros(pgv.shape, F32)
        pgv[0:n_tok, :] = vn_ref[u]
        return x + pltpu.roll(x, n_tok, 0)

    head_lanes = [slice(h * LANES, (h + 1) * LANES) for h in range(DIFF_HEADS)]

    def attention(u, x):
        scores = []
        for sl in head_lanes:
            kt = jnp.concatenate([kbuf[u, r, sl, :].astype(BF16) for r in range(n_pages)]
                                 + [pgk[:, sl].T.astype(BF16)], axis=1)
            q = _rope_slab(x[:, sl], cos, slo, shi) * (DIFF_HEAD_DIM ** -0.5)
            q = jnp.where(own_map, q, 0.0).astype(BF16)
            scores.append(jnp.dot(q, kt, preferred_element_type=F32))
        probs = [_softmax_rows(jnp.where(visible, s, NEG_INF)) for s in scores]
        weights = [(p - lam * pltpu.roll(p, n_tok, 0)).astype(BF16) for p in probs]
        for h, sl in enumerate(head_lanes):
            vv = jnp.concatenate([vbuf[u, r, h].astype(BF16) for r in range(n_pages)]
                                 + [pgv[:, sl].astype(BF16)], axis=0)
            o = jnp.dot(weights[h], vv, preferred_element_type=F32)
            mixs_ref[u, :, sl] = _subln(o[0:n_tok], g_ref, lam_init).astype(mixs_ref.dtype)

    def mlp_piece(c):
        sl = slice(c * FF_PIECE, (c + 1) * FF_PIECE)
        a = jnp.maximum(jnp.dot(hmid, w1_ref[:, sl], preferred_element_type=F32), 0.0)
        o_ref[...] += jnp.dot((a * a).astype(BF16), w2_ref[sl, :], preferred_element_type=F32)

    pieces = D_FF // FF_PIECE // per_step
    for u in range(per_step):
        req = i * per_step + u
        for cp in page_copies(req, u):
            cp.wait()
        attention(u, attention_inputs(u))
        for c in range(pieces):
            mlp_piece(u * pieces + c)
        @pl.when(req + per_step < n_req)
        def _():
            for cp in page_copies(req + per_step, u):
                cp.start()


def _post_attn(x, mix, mem, woa, wob, g2, w1, w2, tm,
               page_table, zb, tabs, k_new, v_new, cache_kt, cache_vh, lams, g, j, lam_init):
    m, d = x.shape
    n_req, n_tok, _ = zb.shape
    n_pages = page_table.shape[1]
    steps = m // tm
    per_step = n_req // steps
    assert per_step * steps == n_req and (D_FF // FF_PIECE) % per_step == 0
    rows = 2 * n_tok

    def const(shape):
        nd = len(shape)
        return pl.BlockSpec(shape, lambda *_: (0,) * nd, pipeline_mode=pl.Buffered(1))

    row = lambda n: pl.BlockSpec((tm, n), lambda i, pt: (i, 0))
    req_spec = pl.BlockSpec((per_step, n_tok, MIX_WIDTH), lambda i, pt: (i, 0, 0))
    lam_spec = const((N_B, DIFF_HEAD_DIM))
    tab_spec = const((rows, LANES))
    grid_spec = pltpu.PrefetchScalarGridSpec(
        num_scalar_prefetch=1,
        grid=(steps,),
        in_specs=[row(d), row(MIX_WIDTH), row(MEM_WIDTH),
                  const((MIX_WIDTH, d)), const((MEM_WIDTH, d)), const((1, d)),
                  const((d, D_FF)), const((D_FF, d)),
                  lam_spec, lam_spec, lam_spec, lam_spec, const((1, DIFF_V_DIM)),
                  req_spec, tab_spec, tab_spec, tab_spec, req_spec, req_spec,
                  pl.BlockSpec(memory_space=pl.ANY), pl.BlockSpec(memory_space=pl.ANY)],
        out_specs=[row(d), req_spec],
        scratch_shapes=[pltpu.VMEM((per_step, n_pages, MIX_WIDTH, PAGE_SIZE), F32),
                        pltpu.VMEM((per_step, n_pages, DIFF_HEADS, PAGE_SIZE, LANES), F32),
                        pltpu.SemaphoreType.DMA((per_step, 2)),
                        pltpu.VMEM((rows, MIX_WIDTH), F32),
                        pltpu.VMEM((PAGE_SIZE, MIX_WIDTH), F32),
                        pltpu.VMEM((PAGE_SIZE, MIX_WIDTH), F32)],
    )
    return pl.pallas_call(
        functools.partial(_post_attn_kernel, j=j, lam_init=lam_init, n_pages=n_pages, n_tok=n_tok,
                          per_step=per_step, n_req=n_req),
        grid_spec=grid_spec,
        out_shape=[jax.ShapeDtypeStruct((m, d), F32),
                   jax.ShapeDtypeStruct((n_req, n_tok, MIX_WIDTH), BF16)],
        compiler_params=pltpu.CompilerParams(dimension_semantics=("arbitrary",),
                                             vmem_limit_bytes=FUSED_VMEM_LIMIT),
        name="post_attn",
    )(page_table, x, mix, mem, woa, wob, g2.reshape(1, d), w1, w2,
      *lams, g.reshape(1, DIFF_V_DIM), zb, *tabs, k_new, v_new, cache_kt, cache_vh)


def _rope_tables(pos):
    inv = 1.0 / (ROPE_THETA ** (jnp.arange(0, DIFF_HEAD_DIM, 2, dtype=F32) / DIFF_HEAD_DIM))
    ang = pos.astype(F32)[:, None] * inv[None, :]
    cos, sin = jnp.cos(ang), jnp.sin(ang)
    zero = jnp.zeros_like(sin)
    tile2 = lambda a, b: jnp.concatenate([a, b, a, b], axis=-1)
    return tile2(cos, cos), tile2(-sin, zero), tile2(zero, sin)


def _lam_init(layer):
    return 0.8 - 0.6 * math.exp(-0.3 * layer)


def kernel(x_prompt, x_sample, state_pool, cache_mem_k, cache_mem_v, cache_k, cache_v, page_table, mem_prompt,
           norm1, norm2, final_norm, kv_norm, mem_norm, w_in, w_out, w_mem_k, w_mem_v, w_pool, pool_scale,
           w_k, w_v, lambda_q1, lambda_k1, lambda_q2, lambda_k2, subln, w_mlp1, w_mlp2):
    bp, tp, d = x_prompt.shape
    bs, ts, _ = x_sample.shape
    n_pages = page_table.shape[1]
    assert d == D_MODEL and n_pages * PAGE_SIZE == PAST_LEN and PAST_LEN >= POOL_HIST

    w_in_b = w_in.astype(BF16)
    woa_b = w_out[:, :MIX_WIDTH].astype(BF16)
    wob_b = w_out[:, MIX_WIDTH:].astype(BF16)
    w1_b = w_mlp1.astype(BF16)
    w2_b = w_mlp2.astype(BF16)
    wk_b = w_k.astype(BF16)
    wv_b = w_v.astype(BF16)
    w_memkv = jnp.concatenate([w_mem_k, w_mem_v], axis=-1)
    w_memkv = w_memkv.transpose(1, 0, 2).reshape(d, DEPTH * 2 * MEM_WIDTH).astype(BF16)
    wbd = jnp.zeros((N_A, MIX_WIDTH, MIX_WIDTH), F32)
    for g in range(len(POOL_WINDOWS)):
        sl = slice(g * GROUP_WIDTH, (g + 1) * GROUP_WIDTH)
        wbd = wbd.at[:, sl, sl].set(w_pool[:, g])
    wbd = wbd.astype(BF16)
    lams = (lambda_q1, lambda_k1, lambda_q2, lambda_k2)

    ms = bs * ts
    pos_s = PAST_LEN + jnp.arange(ts, dtype=jnp.int32)
    tabs_s = _rope_tables(jnp.repeat(pos_s, bs))
    tabs_s2 = _rope_tables(jnp.tile(pos_s, 2))
    ck_t = cache_k.transpose(0, 2, 3, 4, 1).reshape(cache_k.shape[0], MIX_WIDTH, PAGE_SIZE)
    cv_h = cache_v.transpose(0, 2, 1, 3)
    cmk_t = cache_mem_k.transpose(0, 1, 3, 4, 2).reshape(DEPTH, bs, MEM_WIDTH, MEM_TOKENS)
    cmv_t = cache_mem_v.transpose(0, 1, 3, 4, 2).reshape(DEPTH, bs, MEM_WIDTH, MEM_TOKENS)
    hist = state_pool.transpose(0, 2, 1, 3)
    head_of_col = jnp.arange(MEM_WIDTH) // MEM_HEAD_DIM
    head_mask = (head_of_col[None, :] == jnp.arange(MEM_HEADS)[:, None]).astype(F32)
    to_requests = lambda a: a.reshape(ts, bs, -1).transpose(1, 0, 2)
    kv_s = {}

    def sample_tail(l, x, z, mix):
        qm = to_requests(z[:, MIX_WIDTH:])[:, None] * (MEM_HEAD_DIM ** -0.5) * head_mask[None, :, None, :]
        om = _mem_sample(qm.reshape(bs, MEM_HEADS * ts, MEM_WIDTH).astype(BF16), cmk_t, cmv_t, l, 16)
        mem = jnp.sum(om.reshape(bs, MEM_HEADS, ts, MEM_WIDTH) * head_mask[None, :, None, :], axis=1)
        mem = mem.transpose(1, 0, 2).reshape(ms, MEM_WIDTH).astype(BF16)
        gf = final_norm if l == DEPTH - 1 else None
        x = _post(x, mix, mem, woa_b[l], wob_b[l], norm2[l], w1_b[l], w2_b[l], gf, ms // 2)
        if l + 1 == N_A:
            kv_s['kt'], kv_s['vh'], z, kv_s['k'], kv_s['v'] = _kv_proj(
                x, kv_norm, norm1[l + 1], wk_b, wv_b, w_in_b[l + 1], tabs_s, ts, bs, bs, False)
        elif l + 1 < DEPTH:
            z = _norm_mm(x, norm1[l + 1], w_in_b[l + 1], ms)
        return x, z

    xs = x_sample.transpose(1, 0, 2).reshape(ms, d)
    pool_sample = []
    zs = _norm_mm(xs, norm1[0], w_in_b[0], ms)
    for l in range(N_A):
        mix_s, kept = _pool_sample(hist, zs.reshape(ts, bs, d), l, wbd[l], pool_scale[l], 32)
        pool_sample.append(kept)
        xs, zs = sample_tail(l, xs, zs, mix_s.reshape(ms, MIX_WIDTH))
    k_new, v_new = to_requests(kv_s['k']), to_requests(kv_s['v'])

    assert N_B < DEPTH
    tm = 512
    mp = bp * tp
    mem_kt, mem_vt, mem_kt_b, mem_v_b = _mem_proj(mem_prompt.reshape(bp * MEM_TOKENS, d), mem_norm, w_memkv, bp)
    to_heads = lambda a: a.reshape(DEPTH, bp, MEM_HEADS, MEM_HEAD_DIM, MEM_TOKENS).transpose(0, 1, 4, 2, 3)
    mem_k_prompt, mem_v_prompt = to_heads(mem_kt), to_heads(mem_vt)

    tabs_p = _rope_tables(jnp.arange(tp, dtype=jnp.int32))
    x = x_prompt.reshape(mp, d)
    pool_prompt = []
    z = _norm_mm(x, norm1[0], w_in_b[0], tm)
    for l in range(DEPTH):
        if l < N_A:
            pool_prompt.append(z.reshape(bp, tp, d)[:, tp - POOL_HIST:, :MIX_WIDTH])
            mix = _pool_prompt(z, wbd[l], pool_scale[l], tp, tm)
        else:
            j = l - N_A
            mix = _diff_prompt(z, kt_b, vh_b, tabs_p, lams, subln[j], j, _lam_init(l), bp, tp, tm)
        mem = _mem_prompt(z, mem_kt_b, mem_v_b, l, tp, tm)
        if l < N_B:
            ls = N_A + l
            x, mix_s = _post_attn(x, mix, mem, woa_b[l], wob_b[l], norm2[l], w1_b[l], w2_b[l], tm // 2,
                                  page_table, to_requests(zs[:, :MIX_WIDTH]), tabs_s2, k_new, v_new,
                                  ck_t, cv_h, lams, subln[l], l, _lam_init(ls))
            xs, zs = sample_tail(ls, xs, zs, mix_s.transpose(1, 0, 2).reshape(ms, MIX_WIDTH))
        else:
            gf = final_norm if l == DEPTH - 1 else None
            x = _post(x, mix, mem, woa_b[l], wob_b[l], norm2[l], w1_b[l], w2_b[l], gf, tm)
        if l + 1 == N_A:
            kt_p, vh_p, z, kt_b, vh_b = _kv_proj(x, kv_norm, norm1[l + 1], wk_b, wv_b, w_in_b[l + 1],
                                                 tabs_p, bp, tp, tm, True)
        elif l + 1 < DEPTH:
            z = _norm_mm(x, norm1[l + 1], w_in_b[l + 1], tm)
    y_prompt = x.reshape(bp, tp, d)
    pool_prompt = jnp.stack(pool_prompt, axis=0)
    k_prompt = kt_p.reshape(bp, DIFF_HEADS, 2, DIFF_HEAD_DIM, tp).transpose(0, 4, 1, 2, 3)
    v_prompt = vh_p.transpose(0, 2, 1, 3)

    y_sample = xs.reshape(ts, bs, d).transpose(1, 0, 2)
    pool_sample = jnp.stack(pool_sample, axis=0).transpose(0, 2, 1, 3)
    k_sample = kv_s['kt'].reshape(ts, DIFF_HEADS, 2, DIFF_HEAD_DIM, bs).transpose(4, 0, 1, 2, 3)
    v_sample = kv_s['vh'].transpose(2, 0, 1, 3)

    return (y_prompt, y_sample, pool_prompt, mem_k_prompt, mem_v_prompt, k_prompt, v_prompt,
            pool_sample, k_sample, v_sample)
```

```python
import functools
import math

import jax
import jax.numpy as jnp
from jax import lax
from jax.experimental import pallas as pl
from jax.experimental.pallas import tpu as pltpu

D_MODEL = 1024
DEPTH = 4
N_A = DEPTH // 2
N_B = DEPTH - N_A
PAST_LEN = 2048
PAGE_SIZE = 128
MEM_TOKENS = 256
MEM_HEADS = 4
MEM_HEAD_DIM = D_MODEL // 16
MEM_WIDTH = MEM_HEADS * MEM_HEAD_DIM
MIX_WIDTH = D_MODEL - MEM_WIDTH
POOL_WINDOWS = (2, 4, 8, 16)
GROUP_WIDTH = MIX_WIDTH // len(POOL_WINDOWS)
POOL_HIST = max(POOL_WINDOWS) - 1
POOL_PAD = POOL_HIST + 1
DIFF_HEAD_DIM = 64
DIFF_HEADS = MIX_WIDTH // (2 * DIFF_HEAD_DIM)
DIFF_V_DIM = 2 * DIFF_HEAD_DIM
D_FF = 4 * D_MODEL
ROPE_THETA = 10000.0
RMS_EPS = 1e-6
NEG_INF = -1e30

LANES = 128
FF_CHUNK = 1024
FF_PIECE = 1024
VMEM_LIMIT = 56 * 1024 * 1024
FUSED_VMEM_LIMIT = 60 * 1024 * 1024

BF16 = jnp.bfloat16
F32 = jnp.float32


def _cparams(n_axes):
    return pltpu.CompilerParams(dimension_semantics=("arbitrary",) * n_axes,
                                vmem_limit_bytes=VMEM_LIMIT)


def _const_spec(shape):
    nd = len(shape)
    return pl.BlockSpec(shape, lambda *_: (0,) * nd)


def _rms_unit(x):
    return x * lax.rsqrt(jnp.mean(x * x, axis=-1, keepdims=True) + RMS_EPS)


def _rope_slab(x, cos, sin_lo, sin_hi):
    return (x * cos + pltpu.roll(x, LANES - DIFF_HEAD_DIM // 2, 1) * sin_lo
            + pltpu.roll(x, DIFF_HEAD_DIM // 2, 1) * sin_hi)


def _lambda_value(lq1, lk1, lq2, lk2, j, lam_init):
    a = jnp.sum(lq1[j:j + 1, :] * lk1[j:j + 1, :], axis=-1, keepdims=True)
    b = jnp.sum(lq2[j:j + 1, :] * lk2[j:j + 1, :], axis=-1, keepdims=True)
    return jnp.exp(a) - jnp.exp(b) + lam_init


def _group_select(lane, s2, s4, s8, s16):
    return jnp.where(lane < GROUP_WIDTH, s2 * (1.0 / POOL_WINDOWS[0]),
                     jnp.where(lane < 2 * GROUP_WIDTH, s4 * (1.0 / POOL_WINDOWS[1]),
                               jnp.where(lane < 3 * GROUP_WIDTH, s8 * (1.0 / POOL_WINDOWS[2]),
                                         s16 * (1.0 / POOL_WINDOWS[3]))))


def _norm_mm_kernel(x_ref, g_ref, w_ref, o_ref):
    h = _rms_unit(x_ref[...]) * g_ref[...]
    o_ref[...] = jnp.dot(h.astype(BF16), w_ref[...], preferred_element_type=F32)


def _norm_mm(x, g, w, tm):
    m, d = x.shape
    n = w.shape[1]
    return pl.pallas_call(
        _norm_mm_kernel,
        grid=(m // tm,),
        in_specs=[pl.BlockSpec((tm, d), lambda i: (i, 0)),
                  _const_spec((1, d)),
                  _const_spec((d, n))],
        out_specs=pl.BlockSpec((tm, n), lambda i: (i, 0)),
        out_shape=jax.ShapeDtypeStruct((m, n), F32),
        compiler_params=_cparams(1),
        name="norm_mm",
    )(x, g.reshape(1, d), w)


def _mem_proj_kernel(x_ref, g_ref, w_ref, kt_ref, vt_ref, ktb_ref, vb_ref):
    h = (_rms_unit(x_ref[...]) * g_ref[...]).astype(BF16)
    y = jnp.dot(h, w_ref[...], preferred_element_type=F32)
    for l in range(DEPTH):
        yk = y[:, (2 * l) * MEM_WIDTH:(2 * l + 1) * MEM_WIDTH]
        yv = y[:, (2 * l + 1) * MEM_WIDTH:(2 * l + 2) * MEM_WIDTH]
        ykt = yk.T
        kt_ref[l, 0] = ykt
        ktb_ref[l, 0] = ykt.astype(BF16)
        vt_ref[l, 0] = yv.T
        vb_ref[l, 0] = yv.astype(BF16)


def _mem_proj(x, g, w, batch):
    d = x.shape[1]
    spec = pl.BlockSpec((DEPTH, 1, MEM_WIDTH, MEM_TOKENS), lambda b: (0, b, 0, 0))
    spec_n = pl.BlockSpec((DEPTH, 1, MEM_TOKENS, MEM_WIDTH), lambda b: (0, b, 0, 0))
    return pl.pallas_call(
        _mem_proj_kernel,
        grid=(batch,),
        in_specs=[pl.BlockSpec((MEM_TOKENS, d), lambda b: (b, 0)),
                  _const_spec((1, d)),
                  _const_spec((d, w.shape[1]))],
        out_specs=[spec, spec, spec, spec_n],
        out_shape=[jax.ShapeDtypeStruct((DEPTH, batch, MEM_WIDTH, MEM_TOKENS), F32),
                   jax.ShapeDtypeStruct((DEPTH, batch, MEM_WIDTH, MEM_TOKENS), F32),
                   jax.ShapeDtypeStruct((DEPTH, batch, MEM_WIDTH, MEM_TOKENS), BF16),
                   jax.ShapeDtypeStruct((DEPTH, batch, MEM_TOKENS, MEM_WIDTH), BF16)],
        compiler_params=_cparams(1),
        name="mem_proj",
    )(x, g.reshape(1, d), w)


def _kv_kernel(x_ref, gkv_ref, gn_ref, wk_ref, wv_ref, win_ref, cos_ref, slo_ref, shi_ref,
               kt_ref, vh_ref, z_ref, a_ref, b_ref, *, attention_copies):
    y = _rms_unit(x_ref[...])
    hk = (y * gkv_ref[...]).astype(BF16)
    hn = (y * gn_ref[...]).astype(BF16)
    z_ref[...] = jnp.dot(hn, win_ref[...], preferred_element_type=F32)
    v = jnp.dot(hk, wv_ref[...], preferred_element_type=F32)
    kraw = jnp.dot(hk, wk_ref[...], preferred_element_type=F32)
    cos, slo, shi = cos_ref[...], slo_ref[...], shi_ref[...]
    if not attention_copies:
        b_ref[...] = v
    for h in range(DIFF_HEADS):
        sl = slice(h * LANES, (h + 1) * LANES)
        kr = _rope_slab(kraw[:, sl], cos, slo, shi)
        krt = kr.T
        kt_ref[0, sl, :] = krt
        vh_ref[0, h] = v[:, sl]
        if attention_copies:
            a_ref[0, sl, :] = krt.astype(BF16)
            b_ref[0, h] = v[:, sl].astype(BF16)
        else:
            a_ref[:, sl] = kr


def _kv_proj(x, gkv, gn, wk, wv, win, tabs, n_seq, seq, tm, attention_copies):
    m, d = x.shape
    tps = seq // tm
    n_tab = tabs[0].shape[0] // tm
    tab_spec = pl.BlockSpec((tm, LANES), lambda i: (i % n_tab, 0))
    row = lambda n: pl.BlockSpec((tm, n), lambda i: (i, 0))
    kt_spec = pl.BlockSpec((1, MIX_WIDTH, tm), lambda i: (i // tps, 0, i % tps))
    vh_spec = pl.BlockSpec((1, DIFF_HEADS, tm, LANES), lambda i: (i // tps, 0, i % tps, 0))
    out_specs = [kt_spec, vh_spec, row(D_MODEL)]
    out_shape = [jax.ShapeDtypeStruct((n_seq, MIX_WIDTH, seq), F32),
                 jax.ShapeDtypeStruct((n_seq, DIFF_HEADS, seq, LANES), F32),
                 jax.ShapeDtypeStruct((m, D_MODEL), F32)]
    if attention_copies:
        out_specs += [pl.BlockSpec((1, MIX_WIDTH, tm), lambda i: (i, 0, 0)), vh_spec]
        out_shape += [jax.ShapeDtypeStruct((m // tm, MIX_WIDTH, tm), BF16),
                      jax.ShapeDtypeStruct((n_seq, DIFF_HEADS, seq, LANES), BF16)]
    else:
        out_specs += [row(MIX_WIDTH), row(MIX_WIDTH)]
        out_shape += [jax.ShapeDtypeStruct((m, MIX_WIDTH), F32)] * 2
    return pl.pallas_call(
        functools.partial(_kv_kernel, attention_copies=attention_copies),
        grid=(m // tm,),
        in_specs=[row(d), _const_spec((1, d)), _const_spec((1, d)),
                  _const_spec((d, MIX_WIDTH)), _const_spec((d, MIX_WIDTH)), _const_spec((d, D_MODEL)),
                  tab_spec, tab_spec, tab_spec],
        out_specs=out_specs,
        out_shape=out_shape,
        compiler_params=_cparams(1),
        name="kv_proj",
    )(x, gkv.reshape(1, d), gn.reshape(1, d), wk, wv, win, *tabs)


def _pool_prompt_kernel(u_ref, h_ref, w_ref, scale_ref, o_ref, *, tiles_per_seq, tm):
    t0 = (pl.program_id(0) % tiles_per_seq) * tm
    hist = jnp.where(t0 > 0, h_ref[...], 0.0)
    ext = jnp.concatenate([hist, u_ref[...]], axis=0)
    s2 = ext + pltpu.roll(ext, 1, 0)
    s4 = s2 + pltpu.roll(s2, 2, 0)
    s8 = s4 + pltpu.roll(s4, 4, 0)
    s16 = s8 + pltpu.roll(s8, 8, 0)
    pos = t0 - POOL_PAD + lax.broadcasted_iota(jnp.int32, (tm + POOL_PAD, 1), 0)
    inv = [1.0 / jnp.clip(pos + 1, 1, w).astype(F32) for w in POOL_WINDOWS]
    lane = lax.broadcasted_iota(jnp.int32, (1, MIX_WIDTH), 1)
    mean = jnp.where(lane < GROUP_WIDTH, s2 * inv[0],
                     jnp.where(lane < 2 * GROUP_WIDTH, s4 * inv[1],
                               jnp.where(lane < 3 * GROUP_WIDTH, s8 * inv[2], s16 * inv[3])))
    pooled = (mean - ext)[POOL_PAD:]
    mix = jnp.dot(pooled.astype(BF16), w_ref[...], preferred_element_type=F32) * scale_ref[...]
    o_ref[...] = mix.astype(o_ref.dtype)


def _pool_prompt(z, wbd, scale, seq, tm):
    m = z.shape[0]
    hist_blocks = tm // POOL_PAD
    return pl.pallas_call(
        functools.partial(_pool_prompt_kernel, tiles_per_seq=seq // tm, tm=tm),
        grid=(m // tm,),
        in_specs=[pl.BlockSpec((tm, MIX_WIDTH), lambda i: (i, 0)),
                  pl.BlockSpec((POOL_PAD, MIX_WIDTH), lambda i: (jnp.maximum(i * hist_blocks - 1, 0), 0)),
                  _const_spec((MIX_WIDTH, MIX_WIDTH)),
                  _const_spec((1, MIX_WIDTH))],
        out_specs=pl.BlockSpec((tm, MIX_WIDTH), lambda i: (i, 0)),
        out_shape=jax.ShapeDtypeStruct((m, MIX_WIDTH), BF16),
        compiler_params=_cparams(1),
        name="pool_prompt",
    )(z, z, wbd, scale.reshape(1, MIX_WIDTH))


def _pool_sample_kernel(hist_ref, u_ref, w_ref, scale_ref, mix_ref, hist_out_ref, *, n_tok):
    rows = [hist_ref[0, t] for t in range(POOL_HIST)] + [u_ref[t] for t in range(n_tok)]
    lane = lax.broadcasted_iota(jnp.int32, (1, MIX_WIDTH), 1)
    pooled = []
    for t in range(n_tok):
        c = POOL_HIST + t
        back = lambda a, b: functools.reduce(lambda x, y: x + y, [rows[c - i] for i in range(a, b)])
        s2 = back(0, 2)
        s4 = s2 + back(2, 4)
        s8 = s4 + back(4, 8)
        s16 = s8 + back(8, 16)
        pooled.append((_group_select(lane, s2, s4, s8, s16) - rows[c]).astype(BF16))
    nb = pooled[0].shape[0]
    mix = jnp.dot(jnp.concatenate(pooled, axis=0), w_ref[...], preferred_element_type=F32) * scale_ref[...]
    for t in range(n_tok):
        mix_ref[t] = mix[t * nb:(t + 1) * nb].astype(mix_ref.dtype)
    for t in range(POOL_HIST):
        hist_out_ref[t] = rows[t + n_tok]


def _pool_sample(hist, z3, layer, wbd, scale, bb):
    n_tok, nb, _ = z3.shape
    return pl.pallas_call(
        functools.partial(_pool_sample_kernel, n_tok=n_tok),
        grid=(nb // bb,),
        in_specs=[pl.BlockSpec((1, POOL_HIST, bb, MIX_WIDTH), lambda i: (layer, 0, i, 0)),
                  pl.BlockSpec((n_tok, bb, MIX_WIDTH), lambda i: (0, i, 0)),
                  _const_spec((MIX_WIDTH, MIX_WIDTH)),
                  _const_spec((1, MIX_WIDTH))],
        out_specs=[pl.BlockSpec((n_tok, bb, MIX_WIDTH), lambda i: (0, i, 0)),
                   pl.BlockSpec((POOL_HIST, bb, MIX_WIDTH), lambda i: (0, i, 0))],
        out_shape=[jax.ShapeDtypeStruct((n_tok, nb, MIX_WIDTH), BF16),
                   jax.ShapeDtypeStruct((POOL_HIST, nb, MIX_WIDTH), F32)],
        compiler_params=_cparams(1),
        name="pool_sample",
    )(hist, z3, wbd, scale.reshape(1, MIX_WIDTH))


def _softmax_rows(s):
    e = jnp.exp(s - jnp.max(s, axis=-1, keepdims=True))
    return e * (1.0 / jnp.sum(e, axis=-1, keepdims=True))


def _mem_prompt_kernel(q_ref, mkt_ref, mv_ref, o_ref):
    q = q_ref[...] * (MEM_HEAD_DIM ** -0.5)
    mkt = mkt_ref[0, 0]
    mv = mv_ref[0, 0]
    lane = lax.broadcasted_iota(jnp.int32, (1, MEM_WIDTH), 1)
    ps, mvs = [], []
    for h in range(MEM_HEADS):
        head = (lane >= h * MEM_HEAD_DIM) & (lane < (h + 1) * MEM_HEAD_DIM)
        qh = jnp.where(head, q, 0.0).astype(BF16)
        s = jnp.dot(qh, mkt, preferred_element_type=F32)
        ps.append(_softmax_rows(s).astype(BF16))
        mvs.append(jnp.where(head, mv, jnp.zeros_like(mv)))
    o = jnp.dot(jnp.concatenate(ps, axis=1), jnp.concatenate(mvs, axis=0), preferred_element_type=F32)
    o_ref[...] = o.astype(o_ref.dtype)


def _mem_prompt(z, mkt, mv, layer, seq, tq):
    m = z.shape[0]
    tiles_per_seq = seq // tq
    q_col = MIX_WIDTH // MEM_WIDTH
    kv_spec = pl.BlockSpec((1, 1, MEM_WIDTH, MEM_TOKENS), lambda i: (layer, i // tiles_per_seq, 0, 0))
    return pl.pallas_call(
        _mem_prompt_kernel,
        grid=(m // tq,),
        in_specs=[pl.BlockSpec((tq, MEM_WIDTH), lambda i: (i, q_col)), kv_spec, kv_spec],
        out_specs=pl.BlockSpec((tq, MEM_WIDTH), lambda i: (i, 0)),
        out_shape=jax.ShapeDtypeStruct((m, MEM_WIDTH), BF16),
        compiler_params=_cparams(1),
        name="mem_prompt",
    )(z, mkt, mv)


def _mem_sample_kernel(q_ref, mkt_ref, mvt_ref, o_ref):
    q = q_ref[...]
    mkt = mkt_ref[0].astype(BF16)
    mvt = mvt_ref[0].astype(BF16)
    s = jnp.einsum('bqd,bdm->bqm', q, mkt, preferred_element_type=F32)
    p = _softmax_rows(s).astype(BF16)
    o_ref[...] = jnp.einsum('bqm,bdm->bqd', p, mvt, preferred_element_type=F32)


def _mem_sample(q_rows, cache_kt, cache_vt, layer, bb):
    nb, rows, _ = q_rows.shape
    kv_spec = pl.BlockSpec((1, bb, MEM_WIDTH, MEM_TOKENS), lambda i: (layer, i, 0, 0))
    return pl.pallas_call(
        _mem_sample_kernel,
        grid=(nb // bb,),
        in_specs=[pl.BlockSpec((bb, rows, MEM_WIDTH), lambda i: (i, 0, 0)), kv_spec, kv_spec],
        out_specs=pl.BlockSpec((bb, rows, MEM_WIDTH), lambda i: (i, 0, 0)),
        out_shape=jax.ShapeDtypeStruct((nb, rows, MEM_WIDTH), F32),
        compiler_params=_cparams(1),
        name="mem_sample",
    )(q_rows, cache_kt, cache_vt)


def _subln(o, g_ref, lam_init):
    return _rms_unit(o) * g_ref[...] * (1.0 - lam_init)


def _lane_tiles(x):
    return [x[:, c * LANES:(c + 1) * LANES] for c in range(x.shape[1] // LANES)]


def _diff_prompt_kernel(lq1, lk1, lq2, lk2, g_ref, q_ref, cos_ref, slo_ref, shi_ref, kt_ref, v_ref,
                        o_ref, qm0, qm1, s0, s1, m0, m1, l_ref, acc_ref, *, j, lam_init, tq, nq):
    i = pl.program_id(2)

    def prologue(qm, m):
        q = _rope_slab(q_ref[...], cos_ref[...], slo_ref[...], shi_ref[...]) * (DIFF_HEAD_DIM ** -0.5)
        lane = lax.broadcasted_iota(jnp.int32, (1, LANES), 1)
        qm[0] = jnp.where(lane < DIFF_HEAD_DIM, q, 0.0).astype(BF16)
        qm[1] = jnp.where(lane >= DIFF_HEAD_DIM, q, 0.0).astype(BF16)
        m[...] = jnp.full(m.shape, NEG_INF, F32)

    def scores(qm, s_buf, m, kv, masked):
        kt = kt_ref[0, kv]
        s_both = jnp.dot(qm[...].reshape(2 * tq, LANES), kt, preferred_element_type=F32)
        for a in range(2):
            s = s_both[a * tq:(a + 1) * tq]
            if masked:
                row = lax.broadcasted_iota(jnp.int32, (tq, tq), 0)
                col = lax.broadcasted_iota(jnp.int32, (tq, tq), 1)
                s = jnp.where(col <= row, s, NEG_INF)
            s_buf[a, kv] = s
            m[a] = functools.reduce(jnp.maximum, _lane_tiles(s), m[a])

    def finish_max(m):
        for a in range(2):
            m[a] = jnp.broadcast_to(jnp.max(m[a], axis=-1, keepdims=True), (tq, LANES))

    def clear_sums():
        l_ref[...] = jnp.zeros(l_ref.shape, F32)
        acc_ref[...] = jnp.zeros(acc_ref.shape, F32)

    def weigh(s_buf, m, kv):
        v = v_ref[0, 0, pl.ds(pl.multiple_of(kv * tq, tq), tq), :]
        ps = []
        for a in range(2):
            row_max = m[a]
            p = [jnp.exp(t - row_max) for t in _lane_tiles(s_buf[a, kv])]
            l_ref[a] += functools.reduce(lambda x, y: x + y, p)
            ps.append(jnp.concatenate(p, axis=1).astype(BF16))
        pv = jnp.dot(jnp.concatenate(ps, axis=0), v, preferred_element_type=F32)
        for a in range(2):
            acc_ref[a] += pv[a * tq:(a + 1) * tq]

    def epilogue():
        lam = _lambda_value(lq1[...], lk1[...], lq2[...], lk2[...], j, lam_init)
        inv = [1.0 / jnp.sum(l_ref[a], axis=-1, keepdims=True) for a in range(2)]
        o = acc_ref[0] * inv[0] - lam * (acc_ref[1] * inv[1])
        o_ref[...] = _subln(o, g_ref, lam_init).astype(o_ref.dtype)

    def middle(qm, s_cur, m_cur, s_prev, m_prev, odd):
        prologue(qm, m_cur)
        clear_sums()

        def both(kv):
            scores(qm, s_cur, m_cur, kv, False)
            weigh(s_prev, m_prev, kv)

        def body(pair, carry):
            both(2 * pair)
            both(2 * pair + 1)
            return carry

        lax.fori_loop(0, lax.shift_right_logical(i, 1), body, 0)
        if odd:
            both(i - 1)
        scores(qm, s_cur, m_cur, i, True)
        finish_max(m_cur)
        epilogue()

    @pl.when(i == 0)
    def _():
        prologue(qm0, m0)
        scores(qm0, s0, m0, 0, True)
        finish_max(m0)

    @pl.when((i > 0) & (i < nq) & (i % 2 == 0))
    def _():
        middle(qm0, s0, m0, s1, m1, False)

    @pl.when((i < nq) & (i % 2 == 1))
    def _():
        middle(qm1, s1, m1, s0, m0, True)

    @pl.when(i == nq)
    def _():
        s_prev, m_prev = (s1, m1) if nq % 2 == 0 else (s0, m0)
        clear_sums()

        def body(pair, carry):
            weigh(s_prev, m_prev, 2 * pair)
            weigh(s_prev, m_prev, 2 * pair + 1)
            return carry

        lax.fori_loop(0, nq // 2, body, 0)
        if nq % 2:
            weigh(s_prev, m_prev, nq - 1)
        epilogue()


def _diff_prompt(z, kt, vb, tabs, lams, g, j, lam_init, batch, seq, tq):
    m = z.shape[0]
    nq = seq // tq
    lam_spec = _const_spec((N_B, DIFF_HEAD_DIM))
    cur = lambda i: jnp.minimum(i, nq - 1)
    done = lambda i: jnp.maximum(i - 1, 0)
    tab_spec = pl.BlockSpec((tq, LANES), lambda b, h, i: (cur(i), 0))
    return pl.pallas_call(
        functools.partial(_diff_prompt_kernel, j=j, lam_init=lam_init, tq=tq, nq=nq),
        grid=(batch, DIFF_HEADS, nq + 1),
        in_specs=[lam_spec, lam_spec, lam_spec, lam_spec,
                  _const_spec((1, DIFF_V_DIM)),
                  pl.BlockSpec((tq, LANES), lambda b, h, i: (b * nq + cur(i), h)),
                  tab_spec, tab_spec, tab_spec,
                  pl.BlockSpec((1, nq, LANES, tq), lambda b, h, i: (b, 0, h, 0)),
                  pl.BlockSpec((1, 1, seq, LANES), lambda b, h, i: (b, h, 0, 0))],
        out_specs=pl.BlockSpec((tq, LANES), lambda b, h, i: (b * nq + done(i), h)),
        out_shape=jax.ShapeDtypeStruct((m, MIX_WIDTH), BF16),
        scratch_shapes=[pltpu.VMEM((2, tq, LANES), BF16), pltpu.VMEM((2, tq, LANES), BF16),
                        pltpu.VMEM((2, nq, tq, tq), F32), pltpu.VMEM((2, nq, tq, tq), F32),
                        pltpu.VMEM((2, tq, LANES), F32), pltpu.VMEM((2, tq, LANES), F32),
                        pltpu.VMEM((2, tq, LANES), F32),
                        pltpu.VMEM((2, tq, LANES), F32)],
        compiler_params=_cparams(3),
        name="diff_prompt",
    )(*lams, g.reshape(1, DIFF_V_DIM), z, *tabs, kt.reshape(batch, nq, MIX_WIDTH, tq), vb)


def _post_kernel(x_ref, mix_ref, mem_ref, woa_ref, wob_ref, g2_ref, w1_ref, w2_ref, *rest, final):
    if final:
        gf_ref, o_ref = rest
    else:
        (o_ref,) = rest
    o_ref[...] = (x_ref[...]
                  + jnp.dot(mix_ref[...], woa_ref[...], preferred_element_type=F32)
                  + jnp.dot(mem_ref[...], wob_ref[...], preferred_element_type=F32))
    h = (_rms_unit(o_ref[...]) * g2_ref[...]).astype(BF16)
    for c in range(D_FF // FF_CHUNK):
        sl = slice(c * FF_CHUNK, (c + 1) * FF_CHUNK)
        a = jnp.maximum(jnp.dot(h, w1_ref[:, sl], preferred_element_type=F32), 0.0)
        o_ref[...] += jnp.dot((a * a).astype(BF16), w2_ref[sl, :], preferred_element_type=F32)
    if final:
        o_ref[...] = _rms_unit(o_ref[...]) * gf_ref[...]


def _post(x, mix, mem, woa, wob, g2, w1, w2, gf, tm):
    m, d = x.shape
    final = gf is not None
    row = lambda n: pl.BlockSpec((tm, n), lambda i: (i, 0))
    in_specs = [row(d), row(MIX_WIDTH), row(MEM_WIDTH),
                _const_spec((MIX_WIDTH, d)), _const_spec((MEM_WIDTH, d)), _const_spec((1, d)),
                _const_spec((d, D_FF)), _const_spec((D_FF, d))]
    args = [x, mix, mem, woa, wob, g2.reshape(1, d), w1, w2]
    if final:
        in_specs.append(_const_spec((1, d)))
        args.append(gf.reshape(1, d))
    return pl.pallas_call(
        functools.partial(_post_kernel, final=final),
        grid=(m // tm,),
        in_specs=in_specs,
        out_specs=row(d),
        out_shape=jax.ShapeDtypeStruct((m, d), F32),
        compiler_params=_cparams(1),
        name="post_final" if final else "post",
    )(*args)


def _post_attn_kernel(pt_ref, x_ref, mix_ref, mem_ref, woa_ref, wob_ref, g2_ref, w1_ref, w2_ref,
                      lq1, lk1, lq2, lk2, g_ref, z_ref, cos_ref, slo_ref, shi_ref, kn_ref, vn_ref,
                      ck_hbm, cv_hbm, o_ref, mixs_ref,
                      kbuf, vbuf, sem, x_scr, pgk, pgv,
                      *, j, lam_init, n_pages, n_tok, per_step, n_req):
    i = pl.program_id(0)
    past = n_pages * PAGE_SIZE
    n_keys = past + PAGE_SIZE
    rows = 2 * n_tok

    def page_copies(req, slot):
        cps = []
        for r in range(n_pages):
            pg = pt_ref[req, r]
            cps.append(pltpu.make_async_copy(ck_hbm.at[pg], kbuf.at[slot, r], sem.at[slot, 0]))
            cps.append(pltpu.make_async_copy(cv_hbm.at[pg], vbuf.at[slot, r], sem.at[slot, 1]))
        return cps

    @pl.when(i == 0)
    def _():
        for u in range(per_step):
            for cp in page_copies(u, u):
                cp.start()

    o_ref[...] = (x_ref[...]
                  + jnp.dot(mix_ref[...], woa_ref[...], preferred_element_type=F32)
                  + jnp.dot(mem_ref[...], wob_ref[...], preferred_element_type=F32))
    hmid = (_rms_unit(o_ref[...]) * g2_ref[...]).astype(BF16)

    lam = _lambda_value(lq1[...], lk1[...], lq2[...], lk2[...], j, lam_init)
    row = lax.broadcasted_iota(jnp.int32, (rows, LANES), 0)
    lane = lax.broadcasted_iota(jnp.int32, (rows, LANES), 1)
    own_map = (row < n_tok) == (lane < DIFF_HEAD_DIM)
    key = lax.broadcasted_iota(jnp.int32, (rows, n_keys), 1)
    tok = lax.broadcasted_iota(jnp.int32, (rows, n_keys), 0) % n_tok
    visible = key - past <= tok
    cos, slo, shi = cos_ref[...], slo_ref[...], shi_ref[...]

    def attention_inputs(u):
        x_scr[...] = jnp.zeros(x_scr.shape, F32)
        x_scr[0:n_tok, :] = z_ref[u]
        x = x_scr[...]
        pgk[...] = jnp.zeros(pgk.shape, F32)
        pgk[0:n_tok, :] = kn_ref[u]
        pgv[...] = jnp.zeros(pgv.shape, F32)
        pgv[0:n_tok, :] = vn_ref[u]
        return x + pltpu.roll(x, n_tok, 0)

    head_lanes = [slice(h * LANES, (h + 1) * LANES) for h in range(DIFF_HEADS)]

    def attention(u, x):
        scores = []
        for sl in head_lanes:
            kt = jnp.concatenate([kbuf[u, r, sl, :].astype(BF16) for r in range(n_pages)]
                                 + [pgk[:, sl].T.astype(BF16)], axis=1)
            q = _rope_slab(x[:, sl], cos, slo, shi) * (DIFF_HEAD_DIM ** -0.5)
            q = jnp.where(own_map, q, 0.0).astype(BF16)
            scores.append(jnp.dot(q, kt, preferred_element_type=F32))
        probs = [_softmax_rows(jnp.where(visible, s, NEG_INF)) for s in scores]
        weights = [(p - lam * pltpu.roll(p, n_tok, 0)).astype(BF16) for p in probs]
        for h, sl in enumerate(head_lanes):
            vv = jnp.concatenate([vbuf[u, r, h].astype(BF16) for r in range(n_pages)]
                                 + [pgv[:, sl].astype(BF16)], axis=0)
            o = jnp.dot(weights[h], vv, preferred_element_type=F32)
            mixs_ref[u, :, sl] = _subln(o[0:n_tok], g_ref, lam_init).astype(mixs_ref.dtype)

    def mlp_piece(c):
        sl = slice(c * FF_PIECE, (c + 1) * FF_PIECE)
        a = jnp.maximum(jnp.dot(hmid, w1_ref[:, sl], preferred_element_type=F32), 0.0)
        o_ref[...] += jnp.dot((a * a).astype(BF16), w2_ref[sl, :], preferred_element_type=F32)

    pieces = D_FF // FF_PIECE // per_step
    for u in range(per_step):
        req = i * per_step + u
        for cp in page_copies(req, u):
            cp.wait()
        attention(u, attention_inputs(u))
        for c in range(pieces):
            mlp_piece(u * pieces + c)
        @pl.when(req + per_step < n_req)
        def _():
            for cp in page_copies(req + per_step, u):
                cp.start()


def _post_attn(x, mix, mem, woa, wob, g2, w1, w2, tm,
               page_table, zb, tabs, k_new, v_new, cache_kt, cache_vh, lams, g, j, lam_init):
    m, d = x.shape
    n_req, n_tok, _ = zb.shape
    n_pages = page_table.shape[1]
    steps = m // tm
    per_step = n_req // steps
    assert per_step * steps == n_req and (D_FF // FF_PIECE) % per_step == 0
    rows = 2 * n_tok

    def const(shape):
        nd = len(shape)
        return pl.BlockSpec(shape, lambda *_: (0,) * nd, pipeline_mode=pl.Buffered(1))

    row = lambda n: pl.BlockSpec((tm, n), lambda i, pt: (i, 0))
    req_spec = pl.BlockSpec((per_step, n_tok, MIX_WIDTH), lambda i, pt: (i, 0, 0))
    lam_spec = const((N_B, DIFF_HEAD_DIM))
    tab_spec = const((rows, LANES))
    grid_spec = pltpu.PrefetchScalarGridSpec(
        num_scalar_prefetch=1,
        grid=(steps,),
        in_specs=[row(d), row(MIX_WIDTH), row(MEM_WIDTH),
                  const((MIX_WIDTH, d)), const((MEM_WIDTH, d)), const((1, d)),
                  const((d, D_FF)), const((D_FF, d)),
                  lam_spec, lam_spec, lam_spec, lam_spec, const((1, DIFF_V_DIM)),
                  req_spec, tab_spec, tab_spec, tab_spec, req_spec, req_spec,
                  pl.BlockSpec(memory_space=pl.ANY), pl.BlockSpec(memory_space=pl.ANY)],
        out_specs=[row(d), req_spec],
        scratch_shapes=[pltpu.VMEM((per_step, n_pages, MIX_WIDTH, PAGE_SIZE), F32),
                        pltpu.VMEM((per_step, n_pages, DIFF_HEADS, PAGE_SIZE, LANES), F32),
                        pltpu.SemaphoreType.DMA((per_step, 2)),
                        pltpu.VMEM((rows, MIX_WIDTH), F32),
                        pltpu.VMEM((PAGE_SIZE, MIX_WIDTH), F32),
                        pltpu.VMEM((PAGE_SIZE, MIX_WIDTH), F32)],
    )
    return pl.pallas_call(
        functools.partial(_post_attn_kernel, j=j, lam_init=lam_init, n_pages=n_pages, n_tok=n_tok,
                          per_step=per_step, n_req=n_req),
        grid_spec=grid_spec,
        out_shape=[jax.ShapeDtypeStruct((m, d), F32),
                   jax.ShapeDtypeStruct((n_req, n_tok, MIX_WIDTH), BF16)],
        compiler_params=pltpu.CompilerParams(dimension_semantics=("arbitrary",),
                                             vmem_limit_bytes=FUSED_VMEM_LIMIT),
        name="post_attn",
    )(page_table, x, mix, mem, woa, wob, g2.reshape(1, d), w1, w2,
      *lams, g.reshape(1, DIFF_V_DIM), zb, *tabs, k_new, v_new, cache_kt, cache_vh)


def _rope_tables(pos):
    inv = 1.0 / (ROPE_THETA ** (jnp.arange(0, DIFF_HEAD_DIM, 2, dtype=F32) / DIFF_HEAD_DIM))
    ang = pos.astype(F32)[:, None] * inv[None, :]
    cos, sin = jnp.cos(ang), jnp.sin(ang)
    zero = jnp.zeros_like(sin)
    tile2 = lambda a, b: jnp.concatenate([a, b, a, b], axis=-1)
    return tile2(cos, cos), tile2(-sin, zero), tile2(zero, sin)


def _lam_init(layer):
    return 0.8 - 0.6 * math.exp(-0.3 * layer)


def kernel(x_prompt, x_sample, state_pool, cache_mem_k, cache_mem_v, cache_k, cache_v, page_table, mem_prompt,
           norm1, norm2, final_norm, kv_norm, mem_norm, w_in, w_out, w_mem_k, w_mem_v, w_pool, pool_scale,
           w_k, w_v, lambda_q1, lambda_k1, lambda_q2, lambda_k2, subln, w_mlp1, w_mlp2):
    bp, tp, d = x_prompt.shape
    bs, ts, _ = x_sample.shape
    n_pages = page_table.shape[1]
    assert d == D_MODEL and n_pages * PAGE_SIZE == PAST_LEN and PAST_LEN >= POOL_HIST

    w_in_b = w_in.astype(BF16)
    woa_b = w_out[:, :MIX_WIDTH].astype(BF16)
    wob_b = w_out[:, MIX_WIDTH:].astype(BF16)
    w1_b = w_mlp1.astype(BF16)
    w2_b = w_mlp2.astype(BF16)
    wk_b = w_k.astype(BF16)
    wv_b = w_v.astype(BF16)
    w_memkv = jnp.concatenate([w_mem_k, w_mem_v], axis=-1)
    w_memkv = w_memkv.transpose(1, 0, 2).reshape(d, DEPTH * 2 * MEM_WIDTH).astype(BF16)
    wbd = jnp.zeros((N_A, MIX_WIDTH, MIX_WIDTH), F32)
    for g in range(len(POOL_WINDOWS)):
        sl = slice(g * GROUP_WIDTH, (g + 1) * GROUP_WIDTH)
        wbd = wbd.at[:, sl, sl].set(w_pool[:, g])
    wbd = wbd.astype(BF16)
    lams = (lambda_q1, lambda_k1, lambda_q2, lambda_k2)

    ms = bs * ts
    pos_s = PAST_LEN + jnp.arange(ts, dtype=jnp.int32)
    tabs_s = _rope_tables(jnp.repeat(pos_s, bs))
    tabs_s2 = _rope_tables(jnp.tile(pos_s, 2))
    ck_t = cache_k.transpose(0, 2, 3, 4, 1).reshape(cache_k.shape[0], MIX_WIDTH, PAGE_SIZE)
    cv_h = cache_v.transpose(0, 2, 1, 3)
    cmk_t = cache_mem_k.transpose(0, 1, 3, 4, 2).reshape(DEPTH, bs, MEM_WIDTH, MEM_TOKENS)
    cmv_t = cache_mem_v.transpose(0, 1, 3, 4, 2).reshape(DEPTH, bs, MEM_WIDTH, MEM_TOKENS)
    hist = state_pool.transpose(0, 2, 1, 3)
    head_of_col = jnp.arange(MEM_WIDTH) // MEM_HEAD_DIM
    head_mask = (head_of_col[None, :] == jnp.arange(MEM_HEADS)[:, None]).astype(F32)
    to_requests = lambda a: a.reshape(ts, bs, -1).transpose(1, 0, 2)
    kv_s = {}

    def sample_tail(l, x, z, mix):
        qm = to_requests(z[:, MIX_WIDTH:])[:, None] * (MEM_HEAD_DIM ** -0.5) * head_mask[None, :, None, :]
        om = _mem_sample(qm.reshape(bs, MEM_HEADS * ts, MEM_WIDTH).astype(BF16), cmk_t, cmv_t, l, 16)
        mem = jnp.sum(om.reshape(bs, MEM_HEADS, ts, MEM_WIDTH) * head_mask[None, :, None, :], axis=1)
        mem = mem.transpose(1, 0, 2).reshape(ms, MEM_WIDTH).astype(BF16)
        gf = final_norm if l == DEPTH - 1 else None
        x = _post(x, mix, mem, woa_b[l], wob_b[l], norm2[l], w1_b[l], w2_b[l], gf, ms // 2)
        if l + 1 == N_A:
            kv_s['kt'], kv_s['vh'], z, kv_s['k'], kv_s['v'] = _kv_proj(
                x, kv_norm, norm1[l + 1], wk_b, wv_b, w_in_b[l + 1], tabs_s, ts, bs, bs, False)
        elif l + 1 < DEPTH:
            z = _norm_mm(x, norm1[l + 1], w_in_b[l + 1], ms)
        return x, z

    xs = x_sample.transpose(1, 0, 2).reshape(ms, d)
    pool_sample = []
    zs = _norm_mm(xs, norm1[0], w_in_b[0], ms)
    for l in range(N_A):
        mix_s, kept = _pool_sample(hist, zs.reshape(ts, bs, d), l, wbd[l], pool_scale[l], 32)
        pool_sample.append(kept)
        xs, zs = sample_tail(l, xs, zs, mix_s.reshape(ms, MIX_WIDTH))
    k_new, v_new = to_requests(kv_s['k']), to_requests(kv_s['v'])

    assert N_B < DEPTH
    tm = 512
    mp = bp * tp
    mem_kt, mem_vt, mem_kt_b, mem_v_b = _mem_proj(mem_prompt.reshape(bp * MEM_TOKENS, d), mem_norm, w_memkv, bp)
    to_heads = lambda a: a.reshape(DEPTH, bp, MEM_HEADS, MEM_HEAD_DIM, MEM_TOKENS).transpose(0, 1, 4, 2, 3)
    mem_k_prompt, mem_v_prompt = to_heads(mem_kt), to_heads(mem_vt)

    tabs_p = _rope_tables(jnp.arange(tp, dtype=jnp.int32))
    x = x_prompt.reshape(mp, d)
    pool_prompt = []
    z = _norm_mm(x, norm1[0], w_in_b[0], tm)
    for l in range(DEPTH):
        if l < N_A:
            pool_prompt.append(z.reshape(bp, tp, d)[:, tp - POOL_HIST:, :MIX_WIDTH])
            mix = _pool_prompt(z, wbd[l], pool_scale[l], tp, tm)
        else:
            j = l - N_A
            mix = _diff_prompt(z, kt_b, vh_b, tabs_p, lams, subln[j], j, _lam_init(l), bp, tp, tm)
        mem = _mem_prompt(z, mem_kt_b, mem_v_b, l, tp, tm)
        if l < N_B:
            ls = N_A + l
            x, mix_s = _post_attn(x, mix, mem, woa_b[l], wob_b[l], norm2[l], w1_b[l], w2_b[l], tm // 2,
                                  page_table, to_requests(zs[:, :MIX_WIDTH]), tabs_s2, k_new, v_new,
                                  ck_t, cv_h, lams, subln[l], l, _lam_init(ls))
            xs, zs = sample_tail(ls, xs, zs, mix_s.transpose(1, 0, 2).reshape(ms, MIX_WIDTH))
        else:
            gf = final_norm if l == DEPTH - 1 else None
            x = _post(x, mix, mem, woa_b[l], wob_b[l], norm2[l], w1_b[l], w2_b[l], gf, tm)
        if l + 1 == N_A:
            kt_p, vh_p, z, kt_b, vh_b = _kv_proj(x, kv_norm, norm1[l + 1], wk_b, wv_b, w_in_b[l + 1],
                                                 tabs_p, bp, tp, tm, True)
        elif l + 1 < DEPTH:
            z = _norm_mm(x, norm1[l + 1], w_in_b[l + 1], tm)
    y_prompt = x.reshape(bp, tp, d)
    pool_prompt = jnp.stack(pool_prompt, axis=0)
    k_prompt = kt_p.reshape(bp, DIFF_HEADS, 2, DIFF_HEAD_DIM, tp).transpose(0, 4, 1, 2, 3)
    v_prompt = vh_p.transpose(0, 2, 1, 3)

    y_sample = xs.reshape(ts, bs, d).transpose(1, 0, 2)
    pool_sample = jnp.stack(pool_sample, axis=0).transpose(0, 2, 1, 3)
    k_sample = kv_s['kt'].reshape(ts, DIFF_HEADS, 2, DIFF_HEAD_DIM, bs).transpose(4, 0, 1, 2, 3)
    v_sample = kv_s['vh'].transpose(2, 0, 1, 3)

    return (y_prompt, y_sample, pool_prompt, mem_k_prompt, mem_v_prompt, k_prompt, v_prompt,
            pool_sample, k_sample, v_sample)
```
